```python
import jax
import jax.numpy as jnp
from jax import lax
import numpy as np

D_MODEL = 2048
BATCH = 2
SEQ = 8192
DEPTH = 4

N_A_LAYERS = DEPTH // 2
N_B_LAYERS = DEPTH - N_A_LAYERS
RMS_EPS = 1e-6
ROPE_BASE = 10000.0

RET_HEADS = 8
RET_QK_DIM = D_MODEL // RET_HEADS
RET_V_DIM = 2 * D_MODEL // RET_HEADS
RET_CHUNK = 128

MLA_HEADS = D_MODEL // 128
MLA_NOPE_DIM = 128
MLA_ROPE_DIM = 64
MLA_V_DIM = 128
MLA_KV_RANK = 512
MLA_Q_RANK = 3 * MLA_KV_RANK
ATTN_BLOCK = 128

D_FF = 4 * D_MODEL

kernel_name = "yoco_retention_mla_hybrid"


def rms_norm(x, gain):
    xf = x.astype(jnp.float32)
    y = xf * lax.rsqrt(jnp.mean(xf * xf, axis=-1, keepdims=True) + RMS_EPS)
    return (y * gain.astype(jnp.float32)).astype(x.dtype)


def rope_tables(positions, dim):
    inv_freq = 1.0 / (ROPE_BASE ** (jnp.arange(0, dim, 2, dtype=jnp.float32) / dim))
    ang = positions.astype(jnp.float32)[..., None] * inv_freq
    return jnp.cos(ang), jnp.sin(ang)


def apply_rope(x, cos, sin):
    half = x.shape[-1] // 2
    xf = x.astype(jnp.float32)
    x1, x2 = xf[..., :half], xf[..., half:]
    return jnp.concatenate([x1 * cos - x2 * sin, x2 * cos + x1 * sin], axis=-1).astype(x.dtype)


def retention(h, w_in, w_out, cos, sin):
    b, s, _ = h.shape
    H, dk, dv, C = RET_HEADS, RET_QK_DIM, RET_V_DIM, RET_CHUNK
    n_chunks = s // C
    proj = h @ w_in
    q, k, v, gate = jnp.split(proj, [H * dk, 2 * H * dk, 2 * H * dk + H * dv], axis=-1)
    q = apply_rope(q.reshape(b, s, H, dk), cos[:, :, None], sin[:, :, None]).astype(jnp.float32)
    k = apply_rope(k.reshape(b, s, H, dk), cos[:, :, None], sin[:, :, None]).astype(jnp.float32) * (dk ** -0.5)
    v = v.reshape(b, s, H, dv).astype(jnp.float32)

    def to_chunks(t):
        return t.reshape(b, n_chunks, C, H, t.shape[-1]).transpose(0, 3, 1, 2, 4)

    q, k, v = to_chunks(q), to_chunks(k), to_chunks(v)
    log_gamma = jnp.log1p(-jnp.exp2(-5.0 - jnp.arange(H, dtype=jnp.float32)))
    n = jnp.arange(C, dtype=jnp.float32)
    diff = n[:, None] - n[None, :]
    decay = jnp.where(diff >= 0, jnp.exp(jnp.maximum(diff, 0.0) * log_gamma[:, None, None]), 0.0)
    scores = jnp.einsum('bhcnd,bhcmd->bhcnm', q, k) * decay[None, :, None]
    o_intra = jnp.einsum('bhcnm,bhcme->bhcne', scores, v)

    xi = jnp.exp((n + 1.0)[None, :] * log_gamma[:, None])
    zeta = jnp.exp((C - 1.0 - n)[None, :] * log_gamma[:, None])
    chunk_decay = jnp.exp(C * log_gamma)[None, :, None, None]
    qs = (q * xi[None, :, None, :, None]).transpose(2, 0, 1, 3, 4)
    ks = (k * zeta[None, :, None, :, None]).transpose(2, 0, 1, 3, 4)
    vs = v.transpose(2, 0, 1, 3, 4)

    def step(state, inp):
        qc, kc, vc = inp
        out = jnp.einsum('bhnd,bhde->bhne', qc, state)
        state = chunk_decay * state + jnp.einsum('bhmd,bhme->bhde', kc, vc)
        return state, out

    state0 = jnp.zeros((b, H, dk, dv), jnp.float32)
    _, o_cross = lax.scan(step, state0, (qs, ks, vs))
    o = o_intra + o_cross.transpose(1, 2, 0, 3, 4)
    o = o.transpose(0, 2, 3, 1, 4).reshape(b, s, H, dv)
    o = o * lax.rsqrt(jnp.mean(o * o, axis=-1, keepdims=True) + RMS_EPS)
    o = o.reshape(b, s, H * dv).astype(h.dtype)
    return (jax.nn.silu(gate) * o) @ w_out


def mla_shared_kv(h, w_kv_down, kv_norm, w_kv_up, k_nope_norm, k_pe_norm, cos, sin):
    b, s, _ = h.shape
    down = h @ w_kv_down
    c_kv = rms_norm(down[..., :MLA_KV_RANK], kv_norm)
    k_pe = down[..., MLA_KV_RANK:]
    kv = (c_kv @ w_kv_up).reshape(b, s, MLA_HEADS, MLA_NOPE_DIM + MLA_V_DIM)
    k_nope = rms_norm(kv[..., :MLA_NOPE_DIM], k_nope_norm)
    v = kv[..., MLA_NOPE_DIM:]
    k_pe = apply_rope(rms_norm(k_pe, k_pe_norm), cos, sin)
    return k_nope, k_pe, v


def mla_attention(h, w_dq, q_norm, w_uq, q_nope_norm, q_pe_norm, w_o, k_nope, k_pe, v, cos, sin):
    b, s, _ = h.shape
    H = MLA_HEADS
    n_blocks = s // ATTN_BLOCK
    c_q = rms_norm(h @ w_dq, q_norm)
    q = (c_q @ w_uq).reshape(b, s, H, MLA_NOPE_DIM + MLA_ROPE_DIM)
    q_nope = rms_norm(q[..., :MLA_NOPE_DIM], q_nope_norm)
    q_pe = apply_rope(rms_norm(q[..., MLA_NOPE_DIM:], q_pe_norm), cos[:, :, None], sin[:, :, None])
    scale = (MLA_NOPE_DIM + MLA_ROPE_DIM) ** -0.5
    qn_b = q_nope.reshape(b, n_blocks, ATTN_BLOCK, H, MLA_NOPE_DIM).transpose(1, 0, 3, 2, 4)
    qp_b = q_pe.reshape(b, n_blocks, ATTN_BLOCK, H, MLA_ROPE_DIM).transpose(1, 0, 3, 2, 4)
    kpos = jnp.arange(s)

    def attend(args):
        i, qn, qp = args
        sc = jnp.einsum('bhqd,bkhd->bhqk', qn, k_nope) + jnp.einsum('bhqr,bkr->bhqk', qp, k_pe)
        sc = sc.astype(jnp.float32) * scale
        qpos = i * ATTN_BLOCK + jnp.arange(ATTN_BLOCK)
        sc = jnp.where(kpos[None, :] <= qpos[:, None], sc, -jnp.inf)
        p = jax.nn.softmax(sc, axis=-1).astype(v.dtype)
        return jnp.einsum('bhqk,bkhd->bqhd', p, v)

    o = lax.map(attend, (jnp.arange(n_blocks), qn_b, qp_b))
    o = o.transpose(1, 0, 2, 3, 4).reshape(b, s, H * MLA_V_DIM)
    return o @ w_o


def sq_relu_mlp(h, w1, w2):
    return jnp.square(jax.nn.relu(h @ w1)) @ w2


def setup_inputs(seed: int = 0) -> dict:
    key = jax.random.key(seed)
    ks = jax.random.split(key, 24)
    f32 = jnp.float32
    out_gain = (2.0 * DEPTH) ** -0.5

    def w(k, shape, fan_in, gain=1.0):
        return jax.random.normal(k, shape, f32) * (gain * fan_in ** -0.5)

    def g(k, shape):
        return 1.0 + 0.02 * jax.random.normal(k, shape, f32)

    nA, nB = N_A_LAYERS, N_B_LAYERS
    ret_in_width = 2 * RET_HEADS * RET_QK_DIM + 2 * RET_HEADS * RET_V_DIM
    x = jax.random.normal(ks[0], (BATCH, SEQ, D_MODEL), f32)
    offset = jax.random.randint(ks[1], (BATCH, 1), 0, 1024, dtype=jnp.int32)
    positions = offset + jnp.arange(SEQ, dtype=jnp.int32)[None, :]
    return {
        "x": x,
        "positions": positions,
        "norm_mix": g(ks[2], (DEPTH, D_MODEL)),
        "norm_mlp": g(ks[3], (DEPTH, D_MODEL)),
        "ret_w_in": w(ks[4], (nA, D_MODEL, ret_in_width), D_MODEL),
        "ret_w_out": w(ks[5], (nA, RET_HEADS * RET_V_DIM, D_MODEL), RET_HEADS * RET_V_DIM, out_gain),
        "kv_norm_in": g(ks[6], (D_MODEL,)),
        "mla_w_kv_down": w(ks[7], (D_MODEL, MLA_KV_RANK + MLA_ROPE_DIM), D_MODEL),
        "mla_kv_norm": g(ks[8], (MLA_KV_RANK,)),
        "mla_w_kv_up": w(ks[9], (MLA_KV_RANK, MLA_HEADS * (MLA_NOPE_DIM + MLA_V_DIM)), MLA_KV_RANK),
        "mla_k_nope_norm": g(ks[10], (MLA_NOPE_DIM,)),
        "mla_k_pe_norm": g(ks[11], (MLA_ROPE_DIM,)),
        "mla_w_dq": w(ks[12], (nB, D_MODEL, MLA_Q_RANK), D_MODEL),
        "mla_q_norm": g(ks[13], (nB, MLA_Q_RANK)),
        "mla_w_uq": w(ks[14], (nB, MLA_Q_RANK, MLA_HEADS * (MLA_NOPE_DIM + MLA_ROPE_DIM)), MLA_Q_RANK),
        "mla_q_nope_norm": g(ks[15], (nB, MLA_NOPE_DIM)),
        "mla_q_pe_norm": g(ks[16], (nB, MLA_ROPE_DIM)),
        "mla_w_o": w(ks[17], (nB, MLA_HEADS * MLA_V_DIM, D_MODEL), MLA_HEADS * MLA_V_DIM, out_gain),
        "mlp_w1": w(ks[18], (DEPTH, D_MODEL, D_FF), D_MODEL),
        "mlp_w2": w(ks[19], (DEPTH, D_FF, D_MODEL), D_FF, out_gain),
    }


def reference(x, positions, norm_mix, norm_mlp, ret_w_in, ret_w_out, kv_norm_in,
              mla_w_kv_down, mla_kv_norm, mla_w_kv_up, mla_k_nope_norm, mla_k_pe_norm,
              mla_w_dq, mla_q_norm, mla_w_uq, mla_q_nope_norm, mla_q_pe_norm, mla_w_o,
              mlp_w1, mlp_w2):
    ret_cos, ret_sin = rope_tables(positions, RET_QK_DIM)
    mla_cos, mla_sin = rope_tables(positions, MLA_ROPE_DIM)
    k_nope = k_pe = v = None
    for layer in range(DEPTH):
        if layer < N_A_LAYERS:
            h = rms_norm(x, norm_mix[layer])
            x = x + retention(h, ret_w_in[layer], ret_w_out[layer], ret_cos, ret_sin)
        else:
            j = layer - N_A_LAYERS
            if j == 0:
                k_nope, k_pe, v = mla_shared_kv(rms_norm(x, kv_norm_in), mla_w_kv_down, mla_kv_norm,
                                                mla_w_kv_up, mla_k_nope_norm, mla_k_pe_norm,
                                                mla_cos, mla_sin)
            h = rms_norm(x, norm_mix[layer])
            x = x + mla_attention(h, mla_w_dq[j], mla_q_norm[j], mla_w_uq[j], mla_q_nope_norm[j],
                                  mla_q_pe_norm[j], mla_w_o[j], k_nope, k_pe, v, mla_cos, mla_sin)
        x = x + sq_relu_mlp(rms_norm(x, norm_mlp[layer]), mlp_w1[layer], mlp_w2[layer])
    return x
```

```python
import functools

import jax
import jax.numpy as jnp
from jax import lax
from jax.experimental import pallas as pl
from jax.experimental.pallas import tpu as pltpu

F32 = jnp.float32
BF16 = jnp.bfloat16

RMS_EPS = 1e-6
ROPE_BASE = 10000.0

RET_HEADS = 8
RET_QK_DIM = 256
RET_V_DIM = 512
RET_CHUNK = 128

MLA_NOPE_DIM = 128
MLA_ROPE_DIM = 64
MLA_V_DIM = 128
MLA_KV_RANK = 512
MLA_HEAD_PAD = 256

LANES = 128
V7X_VMEM_BYTES = 64 * 1024 * 1024
NEG_BIG = -1e30


def _vmem_limit(block_bytes, scratch_bytes=0, temp_bytes=0):
    need = 2 * block_bytes + scratch_bytes + temp_bytes
    return int(min(max(need + need // 4, 16 * 1024 * 1024), V7X_VMEM_BYTES - 6 * 1024 * 1024))


def _nbytes(shape, dtype):
    n = 1
    for s in shape:
        n *= s
    return n * jnp.dtype(dtype).itemsize


def _rms(x, width):
    ms = jnp.sum(x * x, axis=-1, keepdims=True) * (1.0 / width)
    return x * lax.rsqrt(ms + RMS_EPS)


def _rope64(pe, cos128, sin128):
    lane = lax.broadcasted_iota(jnp.int32, pe.shape, 1)
    half = MLA_ROPE_DIM // 2
    partner = jnp.where(lane < half, pltpu.roll(pe, LANES - half, 1), pltpu.roll(pe, half, 1))
    return pe * cos128 + partner * sin128


def _rmsnorm_kernel(x_ref, g_ref, o_ref):
    x = x_ref[...].astype(F32)
    o_ref[...] = (_rms(x, x.shape[-1]) * g_ref[...]).astype(o_ref.dtype)


def _rmsnorm(x, gain, *, tm=512):
    m, d = x.shape
    return pl.pallas_call(
        _rmsnorm_kernel,
        grid=(m // tm,),
        in_specs=[pl.BlockSpec((tm, d), lambda i: (i, 0)), pl.BlockSpec((1, d), lambda i: (0, 0))],
        out_specs=pl.BlockSpec((tm, d), lambda i: (i, 0)),
        out_shape=jax.ShapeDtypeStruct((m, d), BF16),
        compiler_params=pltpu.CompilerParams(
            dimension_semantics=("parallel",),
            vmem_limit_bytes=_vmem_limit(_nbytes((tm, d), x.dtype) + _nbytes((tm, d), BF16),
                                         temp_bytes=2 * _nbytes((tm, d), F32))),
        name="rmsnorm",
    )(x, gain)


def _mm_kernel(*refs, nk, n_extra, n_out, epilogue):
    a_ref, w_ref = refs[0], refs[1]
    extra = refs[2:2 + n_extra]
    outs = refs[2 + n_extra:2 + n_extra + n_out]
    if nk == 1:
        acc = jnp.dot(a_ref[...], w_ref[...], preferred_element_type=F32)
        epilogue(acc, extra, outs)
        return
    acc_ref = refs[2 + n_extra + n_out]
    k = pl.program_id(2)

    @pl.when(k == 0)
    def _():
        acc_ref[...] = jnp.zeros_like(acc_ref)

    acc_ref[...] += jnp.dot(a_ref[...], w_ref[...], preferred_element_type=F32)

    @pl.when(k == nk - 1)
    def _():
        epilogue(acc_ref[...], extra, outs)


def _matmul(a, w, *, tm, tn, tk, epilogue, extra=(), extra_specs=(), out_shapes, out_specs, name):
    m, kdim = a.shape
    _, n = w.shape
    nk = kdim // tk
    grid = (m // tm, n // tn, nk)
    in_specs = [pl.BlockSpec((tm, tk), lambda i, j, k: (i, k)),
                pl.BlockSpec((tk, tn), lambda i, j, k: (k, j))] + list(extra_specs)
    scratch = [pltpu.VMEM((tm, tn), F32)] if nk > 1 else []
    blk = _nbytes((tm, tk), a.dtype) + _nbytes((tk, tn), w.dtype)
    for arr, spec in zip(extra, extra_specs):
        blk += _nbytes(spec.block_shape, arr.dtype)
    for sh, spec in zip(out_shapes, out_specs):
        blk += _nbytes(spec.block_shape, sh.dtype)
    acc_bytes = _nbytes((tm, tn), F32)
    kern = functools.partial(_mm_kernel, nk=nk, n_extra=len(extra), n_out=len(out_shapes),
                             epilogue=epilogue)
    return pl.pallas_call(
        kern,
        grid=grid,
        in_specs=in_specs,
        out_specs=list(out_specs),
        out_shape=list(out_shapes),
        scratch_shapes=scratch,
        compiler_params=pltpu.CompilerParams(
            dimension_semantics=("parallel", "parallel", "arbitrary"),
            vmem_limit_bytes=_vmem_limit(blk, scratch_bytes=acc_bytes if nk > 1 else 0,
                                         temp_bytes=2 * acc_bytes)),
        name=name,
    )(a, w, *extra)


def _ep_cast(acc, extra, outs):
    outs[0][...] = acc.astype(outs[0].dtype)


def _ep_relu2(acc, extra, outs):
    r = jnp.maximum(acc, 0.0)
    outs[0][...] = (r * r).astype(outs[0].dtype)


def _ep_norm(acc, extra, outs):
    outs[0][...] = (_rms(acc, acc.shape[-1]) * extra[0][...]).astype(outs[0].dtype)


def _ep_residual(acc, extra, outs):
    x_new = extra[0][...] + acc
    outs[0][...] = x_new
    if len(outs) > 1:
        y = _rms(x_new, x_new.shape[-1])
        for g_ref, o_ref in zip(extra[1:], outs[1:]):
            o_ref[...] = (y * g_ref[...]).astype(o_ref.dtype)


def _ep_kv_down(acc, extra, outs):
    g_kv, g_pe, cos_ref, sin_ref = extra
    c = acc[:, :MLA_KV_RANK]
    outs[0][...] = (_rms(c, MLA_KV_RANK) * g_kv[...]).astype(outs[0].dtype)
    pe = acc[:, MLA_KV_RANK:MLA_KV_RANK + LANES]
    pe = _rms(pe, MLA_ROPE_DIM) * g_pe[...]
    outs[1][...] = _rope64(pe, cos_ref[...], sin_ref[...]).astype(outs[1].dtype)


def _ep_kv_up(acc, extra, outs, *, heads):
    g_nope, kpe_ref = extra
    k_out, v_out = outs
    kpe = kpe_ref[...]
    for h in range(heads):
        base = h * (MLA_NOPE_DIM + MLA_V_DIM)
        kn = acc[:, base:base + MLA_NOPE_DIM]
        k_out[:, h * MLA_HEAD_PAD:h * MLA_HEAD_PAD + MLA_NOPE_DIM] = (
            _rms(kn, MLA_NOPE_DIM) * g_nope[...]).astype(k_out.dtype)
        k_out[:, h * MLA_HEAD_PAD + MLA_NOPE_DIM:(h + 1) * MLA_HEAD_PAD] = kpe
        v_out[:, h * MLA_V_DIM:(h + 1) * MLA_V_DIM] = (
            acc[:, base + MLA_NOPE_DIM:base + MLA_NOPE_DIM + MLA_V_DIM]).astype(v_out.dtype)


def _ep_q_up(acc, extra, outs, *, heads, scale):
    g_nope, g_pe, cos_ref, sin_ref = extra
    q_out = outs[0]
    cos128, sin128 = cos_ref[...], sin_ref[...]
    for h in range(heads):
        base = h * MLA_HEAD_PAD
        qn = acc[:, base:base + MLA_NOPE_DIM]
        q_out[:, base:base + MLA_NOPE_DIM] = (
            _rms(qn, MLA_NOPE_DIM) * g_nope[...] * scale).astype(q_out.dtype)
        pe = acc[:, base + MLA_NOPE_DIM:base + MLA_HEAD_PAD]
        pe = _rms(pe, MLA_ROPE_DIM) * g_pe[...]
        q_out[:, base + MLA_NOPE_DIM:base + MLA_HEAD_PAD] = (
            _rope64(pe, cos128, sin128) * scale).astype(q_out.dtype)


def _row_spec(tm, width):
    return pl.BlockSpec((tm, width), lambda i, j, k: (i, 0))


def _gain_spec(width):
    return pl.BlockSpec((1, width), lambda i, j, k: (0, 0))


def _mm_plain(a, w, *, tm, tn, out_dtype, epilogue, name):
    m, n = a.shape[0], w.shape[1]
    return _matmul(a, w, tm=tm, tn=tn, tk=a.shape[1], epilogue=epilogue,
                   out_shapes=[jax.ShapeDtypeStruct((m, n), out_dtype)],
                   out_specs=[pl.BlockSpec((tm, tn), lambda i, j, k: (i, j))], name=name)[0]


def _mm_residual(a, w, res, gains, *, tm, tk, name):
    m, n = a.shape[0], w.shape[1]
    outs = [jax.ShapeDtypeStruct((m, n), F32)] + [jax.ShapeDtypeStruct((m, n), BF16)] * len(gains)
    return _matmul(a, w, tm=tm, tn=n, tk=tk, epilogue=_ep_residual,
                   extra=[res] + list(gains),
                   extra_specs=[_row_spec(tm, n)] + [_gain_spec(n)] * len(gains),
                   out_shapes=outs, out_specs=[_row_spec(tm, n)] * len(outs), name=name)


def _retention_kernel(lg_ref, q_ref, k_ref, v_ref, g_ref, cos_ref, sin_ref, o_ref, state_ref, *,
                      rows):
    c = RET_CHUNK
    half = RET_QK_DIM // 2

    @pl.when(pl.program_id(2) == 0)
    def _():
        state_ref[...] = jnp.zeros_like(state_ref)

    lg = lg_ref[0]
    n_row = lax.broadcasted_iota(jnp.int32, (c, c), 0).astype(F32)
    n_col = lax.broadcasted_iota(jnp.int32, (c, c), 1).astype(F32)
    diff = n_row - n_col
    decay = jnp.where(diff >= 0, jnp.exp(jnp.maximum(diff, 0.0) * lg), 0.0)
    xi = jnp.exp((n_row + 1.0) * lg)
    zeta = jnp.exp((c - 1.0 - n_row) * lg)
    chunk_decay = jnp.exp(c * lg)[:, :1]
    k_scale = RET_QK_DIM ** -0.5

    for ci in range(rows // c):
        sl = slice(ci * c, (ci + 1) * c)
        cos, sin = cos_ref[sl, :], sin_ref[sl, :]
        q = q_ref[sl, :].astype(F32)
        k = k_ref[sl, :].astype(F32)
        q1, q2 = q[:, :half], q[:, half:]
        k1, k2 = k[:, :half], k[:, half:]
        qr1, qr2 = q1 * cos - q2 * sin, q2 * cos + q1 * sin
        kr1, kr2 = (k1 * cos - k2 * sin) * k_scale, (k2 * cos + k1 * sin) * k_scale
        qr = jnp.concatenate([qr1, qr2], axis=1).astype(BF16)
        kr = jnp.concatenate([kr1, kr2], axis=1).astype(BF16)
        qs = jnp.concatenate([qr1 * xi, qr2 * xi], axis=1).astype(BF16)
        ks = jnp.concatenate([kr1 * zeta, kr2 * zeta], axis=1).astype(BF16)
        v = v_ref[sl, :]

        scores = lax.dot_general(qr, kr, (((1,), (1,)), ((), ())), preferred_element_type=F32)
        scores = scores * decay
        state = state_ref[...]
        o = jnp.dot(scores.astype(BF16), v, preferred_element_type=F32)
        o = o + jnp.dot(qs, state.astype(BF16), preferred_element_type=F32)
        state_ref[...] = chunk_decay * state + lax.dot_general(
            ks, v, (((0,), (0,)), ((), ())), preferred_element_type=F32)

        o = _rms(o, RET_V_DIM)
        gate = g_ref[sl, :].astype(F32)
        o_ref[sl, :] = (gate * jax.nn.sigmoid(gate) * o).astype(o_ref.dtype)


def _retention(proj, cos, sin, log_gamma, *, batch, seq, rows=1024):
    m = proj.shape[0]
    nt = seq // rows
    hq = RET_HEADS * RET_QK_DIM // RET_QK_DIM
    hv = 2 * RET_HEADS * RET_QK_DIM // RET_V_DIM
    row = lambda b, h, t: b * nt + t
    in_specs = [
        pl.BlockSpec((1, 1, LANES), lambda b, h, t: (h, 0, 0)),
        pl.BlockSpec((rows, RET_QK_DIM), lambda b, h, t: (row(b, h, t), h)),
        pl.BlockSpec((rows, RET_QK_DIM), lambda b, h, t: (row(b, h, t), hq + h)),
        pl.BlockSpec((rows, RET_V_DIM), lambda b, h, t: (row(b, h, t), hv + h)),
        pl.BlockSpec((rows, RET_V_DIM), lambda b, h, t: (row(b, h, t), hv + RET_HEADS + h)),
        pl.BlockSpec((rows, LANES), lambda b, h, t: (row(b, h, t), 0)),
        pl.BlockSpec((rows, LANES), lambda b, h, t: (row(b, h, t), 0)),
    ]
    blk = (2 * _nbytes((rows, RET_QK_DIM), BF16) + 3 * _nbytes((rows, RET_V_DIM), BF16)
           + 2 * _nbytes((rows, LANES), F32))
    return pl.pallas_call(
        functools.partial(_retention_kernel, rows=rows),
        grid=(batch, RET_HEADS, nt),
        in_specs=in_specs,
        out_specs=pl.BlockSpec((rows, RET_V_DIM), lambda b, h, t: (row(b, h, t), h)),
        out_shape=jax.ShapeDtypeStruct((m, RET_HEADS * RET_V_DIM), BF16),
        scratch_shapes=[pltpu.VMEM((RET_QK_DIM, RET_V_DIM), F32)],
        compiler_params=pltpu.CompilerParams(
            dimension_semantics=("parallel", "parallel", "arbitrary"),
            vmem_limit_bytes=_vmem_limit(blk, scratch_bytes=_nbytes((RET_QK_DIM, RET_V_DIM), F32),
                                         temp_bytes=8 * 1024 * 1024)),
        name="retention",
    )(log_gamma, proj, proj, proj, proj, cos, sin)


def _flash_kernel(q_ref, k_ref, v_ref, o_ref, m_ref, l_ref, acc_ref, *, tq, tk):
    qi = pl.program_id(2)
    q = q_ref[...]
    m_ref[...] = jnp.full_like(m_ref, -jnp.inf)
    l_ref[...] = jnp.zeros_like(l_ref)
    acc_ref[...] = jnp.zeros_like(acc_ref)

    def step(start, masked):
        k = k_ref[pl.ds(start, tk), :]
        v = v_ref[pl.ds(start, tk), :]
        s = lax.dot_general(q, k, (((1,), (1,)), ((), ())), preferred_element_type=F32)
        if masked:
            row = qi * tq + lax.broadcasted_iota(jnp.int32, (tq, tk), 0)
            col = start + lax.broadcasted_iota(jnp.int32, (tq, tk), 1)
            s = jnp.where(col <= row, s, NEG_BIG)
        m_prev = m_ref[...]
        m_new = jnp.maximum(m_prev, jnp.max(s, axis=-1, keepdims=True))
        alpha = jnp.exp(m_prev - m_new)
        p = jnp.exp(s - m_new)
        l_ref[...] = alpha * l_ref[...] + jnp.sum(p, axis=-1, keepdims=True)
        acc_ref[...] = alpha * acc_ref[...] + jnp.dot(p.astype(v.dtype), v,
                                                      preferred_element_type=F32)
        m_ref[...] = m_new

    def full_step(i, carry):
        step(pl.multiple_of(i * tk, tk), False)
        return carry

    lax.fori_loop(0, qi * (tq // tk), full_step, 0)
    for d in range(tq // tk):
        step(pl.multiple_of(qi * tq + d * tk, tk), True)
    o_ref[...] = (acc_ref[...] / l_ref[...]).astype(o_ref.dtype)


def _flash(q, k, v, *, batch, seq, heads, tq=512, tk=512):
    m = q.shape[0]
    nq = seq // tq
    blk = (_nbytes((tq, MLA_HEAD_PAD), BF16) + _nbytes((seq, MLA_HEAD_PAD), BF16)
           + _nbytes((seq, MLA_V_DIM), BF16) + _nbytes((tq, MLA_V_DIM), BF16))
    return pl.pallas_call(
        functools.partial(_flash_kernel, tq=tq, tk=tk),
        grid=(batch, heads, nq),
        in_specs=[
            pl.BlockSpec((tq, MLA_HEAD_PAD), lambda b, h, i: (b * nq + i, h)),
            pl.BlockSpec((seq, MLA_HEAD_PAD), lambda b, h, i: (b, h)),
            pl.BlockSpec((seq, MLA_V_DIM), lambda b, h, i: (b, h)),
        ],
        out_specs=pl.BlockSpec((tq, MLA_V_DIM), lambda b, h, i: (b * nq + i, h)),
        out_shape=jax.ShapeDtypeStruct((m, heads * MLA_V_DIM), BF16),
        scratch_shapes=[pltpu.VMEM((tq, 1), F32), pltpu.VMEM((tq, 1), F32),
                        pltpu.VMEM((tq, MLA_V_DIM), F32)],
        compiler_params=pltpu.CompilerParams(
            dimension_semantics=("parallel", "parallel", "arbitrary"),
            vmem_limit_bytes=_vmem_limit(blk, temp_bytes=6 * _nbytes((tq, tk), F32))),
        name="flash_attention",
    )(q, k, v)


def _rope_tables(positions, dim):
    inv_freq = 1.0 / (ROPE_BASE ** (jnp.arange(0, dim, 2, dtype=F32) / dim))
    ang = positions.astype(F32)[..., None] * inv_freq
    return jnp.cos(ang), jnp.sin(ang)


def kernel(x, positions, norm_mix, norm_mlp, ret_w_in, ret_w_out, kv_norm_in, mla_w_kv_down,
           mla_kv_norm, mla_w_kv_up, mla_k_nope_norm, mla_k_pe_norm, mla_w_dq, mla_q_norm,
           mla_w_uq, mla_q_nope_norm, mla_q_pe_norm, mla_w_o, mlp_w1, mlp_w2):
    batch, seq, d_model = x.shape
    m = batch * seq
    depth = norm_mix.shape[0]
    n_ret = ret_w_in.shape[0]
    mla_heads = mla_w_o.shape[1] // MLA_V_DIM
    row = lambda g: g.reshape(1, -1).astype(F32)

    rc, rs = _rope_tables(positions, RET_QK_DIM)
    ret_cos, ret_sin = rc.reshape(m, -1), rs.reshape(m, -1)
    mc, ms_ = _rope_tables(positions, MLA_ROPE_DIM)
    mc, ms_ = mc.reshape(m, -1), ms_.reshape(m, -1)
    zpad = jnp.zeros((m, LANES - MLA_ROPE_DIM), F32)
    mla_cos = jnp.concatenate([mc, mc, zpad], axis=1)
    mla_sin = jnp.concatenate([-ms_, ms_, zpad], axis=1)
    log_gamma = jnp.log1p(-jnp.exp2(-5.0 - jnp.arange(RET_HEADS, dtype=F32)))
    log_gamma = jnp.broadcast_to(log_gamma[:, None, None], (RET_HEADS, 1, LANES))

    def pe_gain(g):
        return jnp.concatenate([g.astype(F32), jnp.zeros((LANES - MLA_ROPE_DIM,), F32)])[None, :]

    xf = x.reshape(m, d_model)
    h = _rmsnorm(xf, row(norm_mix[0]))
    h_kv = None

    def mlp(xf, h, layer, next_gains):
        u = _mm_plain(h, mlp_w1[layer].astype(BF16), tm=1024, tn=1024, out_dtype=BF16,
                      epilogue=_ep_relu2, name="mlp_up")
        return _mm_residual(u, mlp_w2[layer].astype(BF16), xf, next_gains, tm=512, tk=2048,
                            name="mlp_down")

    for layer in range(depth):
        if layer < n_ret:
            proj = _mm_plain(h, ret_w_in[layer].astype(BF16), tm=1024, tn=1024, out_dtype=BF16,
                             epilogue=_ep_cast, name="ret_proj")
            r = _retention(proj, ret_cos, ret_sin, log_gamma, batch=batch, seq=seq)
            xf, h = _mm_residual(r, ret_w_out[layer].astype(BF16), xf, [row(norm_mlp[layer])],
                                 tm=512, tk=2048, name="ret_out")
        else:
            j = layer - n_ret
            if j == 0:
                w_down = jnp.pad(mla_w_kv_down, ((0, 0), (0, LANES - MLA_ROPE_DIM))).astype(BF16)
                n_down = w_down.shape[1]
                c_kv, k_pe = _matmul(
                    h_kv, w_down, tm=1024, tn=n_down, tk=d_model, epilogue=_ep_kv_down,
                    extra=[row(mla_kv_norm), pe_gain(mla_k_pe_norm), mla_cos, mla_sin],
                    extra_specs=[_gain_spec(MLA_KV_RANK), _gain_spec(LANES),
                                 _row_spec(1024, LANES), _row_spec(1024, LANES)],
                    out_shapes=[jax.ShapeDtypeStruct((m, MLA_KV_RANK), BF16),
                                jax.ShapeDtypeStruct((m, LANES), BF16)],
                    out_specs=[_row_spec(1024, MLA_KV_RANK), _row_spec(1024, LANES)],
                    name="kv_down")
                hpt = 4
                k_all, v_all = _matmul(
                    c_kv, mla_w_kv_up.astype(BF16), tm=1024,
                    tn=hpt * (MLA_NOPE_DIM + MLA_V_DIM), tk=MLA_KV_RANK,
                    epilogue=functools.partial(_ep_kv_up, heads=hpt),
                    extra=[row(mla_k_nope_norm), k_pe],
                    extra_specs=[_gain_spec(MLA_NOPE_DIM), _row_spec(1024, LANES)],
                    out_shapes=[jax.ShapeDtypeStruct((m, mla_heads * MLA_HEAD_PAD), BF16),
                                jax.ShapeDtypeStruct((m, mla_heads * MLA_V_DIM), BF16)],
                    out_specs=[pl.BlockSpec((1024, hpt * MLA_HEAD_PAD), lambda i, j, k: (i, j)),
                               pl.BlockSpec((1024, hpt * MLA_V_DIM), lambda i, j, k: (i, j))],
                    name="kv_up")
            q_rank = mla_w_dq.shape[2]
            c_q = _matmul(h, mla_w_dq[j].astype(BF16), tm=1024, tn=q_rank, tk=d_model,
                          epilogue=_ep_norm, extra=[row(mla_q_norm[j])],
                          extra_specs=[_gain_spec(q_rank)],
                          out_shapes=[jax.ShapeDtypeStruct((m, q_rank), BF16)],
                          out_specs=[_row_spec(1024, q_rank)], name="q_down")[0]
            qk_dim = MLA_NOPE_DIM + MLA_ROPE_DIM
            w_uq = mla_w_uq[j].reshape(q_rank, mla_heads, qk_dim)
            w_uq = jnp.pad(w_uq, ((0, 0), (0, 0), (0, MLA_HEAD_PAD - qk_dim)))
            w_uq = w_uq.reshape(q_rank, mla_heads * MLA_HEAD_PAD).astype(BF16)
            hpt = 4
            q_all = _matmul(
                c_q, w_uq, tm=1024, tn=hpt * MLA_HEAD_PAD, tk=q_rank,
                epilogue=functools.partial(_ep_q_up, heads=hpt, scale=qk_dim ** -0.5),
                extra=[row(mla_q_nope_norm[j]), pe_gain(mla_q_pe_norm[j]), mla_cos, mla_sin],
                extra_specs=[_gain_spec(MLA_NOPE_DIM), _gain_spec(LANES),
                             _row_spec(1024, LANES), _row_spec(1024, LANES)],
                out_shapes=[jax.ShapeDtypeStruct((m, mla_heads * MLA_HEAD_PAD), BF16)],
                out_specs=[pl.BlockSpec((1024, hpt * MLA_HEAD_PAD), lambda i, j, k: (i, j))],
                name="q_up")[0]
            o = _flash(q_all, k_all, v_all, batch=batch, seq=seq, heads=mla_heads)
            xf, h = _mm_residual(o, mla_w_o[j].astype(BF16), xf, [row(norm_mlp[layer])],
                                 tm=512, tk=d_model, name="attn_out")

        if layer + 1 == depth:
            next_gains = []
        elif layer + 1 == n_ret:
            next_gains = [row(norm_mix[layer + 1]), row(kv_norm_in)]
        else:
            next_gains = [row(norm_mix[layer + 1])]
        outs = mlp(xf, h, layer, next_gains)
        xf = outs[0]
        if layer + 1 == n_ret:
            h, h_kv = outs[1], outs[2]
        elif layer + 1 < depth:
            h = outs[1]

    return xf.reshape(batch, seq, d_model)
```

```python
import functools

import jax
import jax.numpy as jnp
from jax import lax
from jax.experimental import pallas as pl
from jax.experimental.pallas import tpu as pltpu

F32 = jnp.float32
BF16 = jnp.bfloat16

RMS_EPS = 1e-6
ROPE_BASE = 10000.0

RET_HEADS = 8
RET_QK_DIM = 256
RET_V_DIM = 512
RET_CHUNK = 128

MLA_NOPE_DIM = 128
MLA_ROPE_DIM = 64
MLA_V_DIM = 128
MLA_KV_RANK = 512
MLA_HEAD_PAD = 256

LANES = 128
V7X_VMEM_BYTES = 64 * 1024 * 1024
NEG_BIG = -1e30
LOG2_E = 1.4426950408889634


def _vmem_limit(block_bytes, scratch_bytes=0, temp_bytes=0):
    need = 2 * block_bytes + scratch_bytes + temp_bytes
    return int(min(max(need + need // 4, 16 * 1024 * 1024), V7X_VMEM_BYTES - 6 * 1024 * 1024))


def _nbytes(shape, dtype):
    n = 1
    for s in shape:
        n *= s
    return n * jnp.dtype(dtype).itemsize


def _rms(x, width):
    ms = jnp.sum(x * x, axis=-1, keepdims=True) * (1.0 / width)
    return x * lax.rsqrt(ms + RMS_EPS)


def _rope64(pe, cos128, sin128):
    lane = lax.broadcasted_iota(jnp.int32, pe.shape, 1)
    half = MLA_ROPE_DIM // 2
    partner = jnp.where(lane < half, pltpu.roll(pe, LANES - half, 1), pltpu.roll(pe, half, 1))
    return pe * cos128 + partner * sin128


def _rmsnorm_kernel(x_ref, g_ref, o_ref):
    x = x_ref[...].astype(F32)
    o_ref[...] = (_rms(x, x.shape[-1]) * g_ref[...]).astype(o_ref.dtype)


def _rmsnorm(x, gain, *, tm=512):
    m, d = x.shape
    return pl.pallas_call(
        _rmsnorm_kernel,
        grid=(m // tm,),
        in_specs=[pl.BlockSpec((tm, d), lambda i: (i, 0)), pl.BlockSpec((1, d), lambda i: (0, 0))],
        out_specs=pl.BlockSpec((tm, d), lambda i: (i, 0)),
        out_shape=jax.ShapeDtypeStruct((m, d), BF16),
        compiler_params=pltpu.CompilerParams(
            dimension_semantics=("parallel",),
            vmem_limit_bytes=_vmem_limit(_nbytes((tm, d), x.dtype) + _nbytes((tm, d), BF16),
                                         temp_bytes=2 * _nbytes((tm, d), F32))),
        name="rmsnorm",
    )(x, gain)


def _mm_kernel(*refs, nk, n_extra, n_out, epilogue):
    a_ref, w_ref = refs[0], refs[1]
    extra = refs[2:2 + n_extra]
    outs = refs[2 + n_extra:2 + n_extra + n_out]
    if nk == 1:
        acc = jnp.dot(a_ref[...], w_ref[...], preferred_element_type=F32)
        epilogue(acc, extra, outs)
        return
    acc_ref = refs[2 + n_extra + n_out]
    k = pl.program_id(2)

    @pl.when(k == 0)
    def _():
        acc_ref[...] = jnp.zeros_like(acc_ref)

    acc_ref[...] += jnp.dot(a_ref[...], w_ref[...], preferred_element_type=F32)

    @pl.when(k == nk - 1)
    def _():
        epilogue(acc_ref[...], extra, outs)


def _matmul(a, w, *, tm, tn, tk, epilogue, extra=(), extra_specs=(), out_shapes, out_specs, name):
    m, kdim = a.shape
    _, n = w.shape
    nk = kdim // tk
    grid = (m // tm, n // tn, nk)
    in_specs = [pl.BlockSpec((tm, tk), lambda i, j, k: (i, k)),
                pl.BlockSpec((tk, tn), lambda i, j, k: (k, j))] + list(extra_specs)
    scratch = [pltpu.VMEM((tm, tn), F32)] if nk > 1 else []
    blk = _nbytes((tm, tk), a.dtype) + _nbytes((tk, tn), w.dtype)
    for arr, spec in zip(extra, extra_specs):
        blk += _nbytes(spec.block_shape, arr.dtype)
    for sh, spec in zip(out_shapes, out_specs):
        blk += _nbytes(spec.block_shape, sh.dtype)
    acc_bytes = _nbytes((tm, tn), F32)
    kern = functools.partial(_mm_kernel, nk=nk, n_extra=len(extra), n_out=len(out_shapes),
                             epilogue=epilogue)
    return pl.pallas_call(
        kern,
        grid=grid,
        in_specs=in_specs,
        out_specs=list(out_specs),
        out_shape=list(out_shapes),
        scratch_shapes=scratch,
        compiler_params=pltpu.CompilerParams(
            dimension_semantics=("parallel", "parallel", "arbitrary"),
            vmem_limit_bytes=_vmem_limit(blk, scratch_bytes=acc_bytes if nk > 1 else 0,
                                         temp_bytes=2 * acc_bytes)),
        name=name,
    )(a, w, *extra)


def _ep_cast(acc, extra, outs):
    outs[0][...] = acc.astype(outs[0].dtype)


def _ep_relu2(acc, extra, outs):
    r = jnp.maximum(acc, 0.0)
    outs[0][...] = (r * r).astype(outs[0].dtype)


def _ep_norm(acc, extra, outs):
    outs[0][...] = (_rms(acc, acc.shape[-1]) * extra[0][...]).astype(outs[0].dtype)


def _ep_residual(acc, extra, outs):
    x_new = extra[0][...] + acc
    outs[0][...] = x_new
    if len(outs) > 1:
        y = _rms(x_new, x_new.shape[-1])
        for g_ref, o_ref in zip(extra[1:], outs[1:]):
            o_ref[...] = (y * g_ref[...]).astype(o_ref.dtype)


def _ep_kv_down(acc, extra, outs):
    g_kv, g_pe, cos_ref, sin_ref = extra
    c = acc[:, :MLA_KV_RANK]
    outs[0][...] = (_rms(c, MLA_KV_RANK) * g_kv[...]).astype(outs[0].dtype)
    pe = acc[:, MLA_KV_RANK:MLA_KV_RANK + LANES]
    pe = _rms(pe, MLA_ROPE_DIM) * g_pe[...]
    outs[1][...] = _rope64(pe, cos_ref[...], sin_ref[...]).astype(outs[1].dtype)


def _ep_kv_up(acc, extra, outs, *, heads):
    g_nope, kpe_ref = extra
    k_out, v_out = outs
    kpe = kpe_ref[...]
    for h in range(heads):
        base = h * (MLA_NOPE_DIM + MLA_V_DIM)
        kn = acc[:, base:base + MLA_NOPE_DIM]
        k_out[:, h * MLA_HEAD_PAD:h * MLA_HEAD_PAD + MLA_NOPE_DIM] = (
            _rms(kn, MLA_NOPE_DIM) * g_nope[...]).astype(k_out.dtype)
        k_out[:, h * MLA_HEAD_PAD + MLA_NOPE_DIM:(h + 1) * MLA_HEAD_PAD] = kpe
        v_out[:, h * MLA_V_DIM:(h + 1) * MLA_V_DIM] = (
            acc[:, base + MLA_NOPE_DIM:base + MLA_NOPE_DIM + MLA_V_DIM]).astype(v_out.dtype)


def _ep_q_up(acc, extra, outs, *, heads, scale):
    g_nope, g_pe, cos_ref, sin_ref = extra
    q_out = outs[0]
    cos128, sin128 = cos_ref[...], sin_ref[...]
    for h in range(heads):
        base = h * MLA_HEAD_PAD
        qn = acc[:, base:base + MLA_NOPE_DIM]
        q_out[:, base:base + MLA_NOPE_DIM] = (
            _rms(qn, MLA_NOPE_DIM) * g_nope[...] * scale).astype(q_out.dtype)
        pe = acc[:, base + MLA_NOPE_DIM:base + MLA_HEAD_PAD]
        pe = _rms(pe, MLA_ROPE_DIM) * g_pe[...]
        q_out[:, base + MLA_NOPE_DIM:base + MLA_HEAD_PAD] = (
            _rope64(pe, cos128, sin128) * scale).astype(q_out.dtype)


def _row_spec(tm, width):
    return pl.BlockSpec((tm, width), lambda i, j, k: (i, 0))


def _gain_spec(width):
    return pl.BlockSpec((1, width), lambda i, j, k: (0, 0))


def _mm_plain(a, w, *, tm, tn, out_dtype, epilogue, name):
    m, n = a.shape[0], w.shape[1]
    return _matmul(a, w, tm=tm, tn=tn, tk=a.shape[1], epilogue=epilogue,
                   out_shapes=[jax.ShapeDtypeStruct((m, n), out_dtype)],
                   out_specs=[pl.BlockSpec((tm, tn), lambda i, j, k: (i, j))], name=name)[0]


def _mm_residual(a, w, res, gains, *, tm, tk, name):
    m, n = a.shape[0], w.shape[1]
    outs = [jax.ShapeDtypeStruct((m, n), F32)] + [jax.ShapeDtypeStruct((m, n), BF16)] * len(gains)
    return _matmul(a, w, tm=tm, tn=n, tk=tk, epilogue=_ep_residual,
                   extra=[res] + list(gains),
                   extra_specs=[_row_spec(tm, n)] + [_gain_spec(n)] * len(gains),
                   out_shapes=outs, out_specs=[_row_spec(tm, n)] * len(outs), name=name)


def _retention_kernel(lg_ref, q_ref, k_ref, v_ref, g_ref, cos_ref, sin_ref, o_ref, state_ref, *,
                      rows):
    c = RET_CHUNK
    half = RET_QK_DIM // 2

    @pl.when(pl.program_id(2) == 0)
    def _():
        state_ref[...] = jnp.zeros_like(state_ref)

    lg = lg_ref[0]
    n_row = lax.broadcasted_iota(jnp.int32, (c, c), 0).astype(F32)
    n_col = lax.broadcasted_iota(jnp.int32, (c, c), 1).astype(F32)
    diff = n_row - n_col
    decay = jnp.where(diff >= 0, jnp.exp(jnp.maximum(diff, 0.0) * lg), 0.0)
    xi = jnp.exp((n_row + 1.0) * lg)
    zeta = jnp.exp((c - 1.0 - n_row) * lg)
    chunk_decay = jnp.exp(c * lg)[:, :1]
    k_scale = RET_QK_DIM ** -0.5

    for ci in range(rows // c):
        sl = slice(ci * c, (ci + 1) * c)
        cos, sin = cos_ref[sl, :], sin_ref[sl, :]
        q = q_ref[sl, :].astype(F32)
        k = k_ref[sl, :].astype(F32)
        q1, q2 = q[:, :half], q[:, half:]
        k1, k2 = k[:, :half], k[:, half:]
        qr1, qr2 = q1 * cos - q2 * sin, q2 * cos + q1 * sin
        kr1, kr2 = (k1 * cos - k2 * sin) * k_scale, (k2 * cos + k1 * sin) * k_scale
        qr = jnp.concatenate([qr1, qr2], axis=1).astype(BF16)
        kr = jnp.concatenate([kr1, kr2], axis=1).astype(BF16)
        qs = jnp.concatenate([qr1 * xi, qr2 * xi], axis=1).astype(BF16)
        ks = jnp.concatenate([kr1 * zeta, kr2 * zeta], axis=1).astype(BF16)
        v = v_ref[sl, :]

        scores = lax.dot_general(qr, kr, (((1,), (1,)), ((), ())), preferred_element_type=F32)
        scores = scores * decay
        state = state_ref[...]
        o = jnp.dot(scores.astype(BF16), v, preferred_element_type=F32)
        o = o + jnp.dot(qs, state.astype(BF16), preferred_element_type=F32)
        state_ref[...] = chunk_decay * state + lax.dot_general(
            ks, v, (((0,), (0,)), ((), ())), preferred_element_type=F32)

        o = _rms(o, RET_V_DIM)
        gate = g_ref[sl, :].astype(F32)
        o_ref[sl, :] = (gate * jax.nn.sigmoid(gate) * o).astype(o_ref.dtype)


def _retention(proj, cos, sin, log_gamma, *, batch, seq, rows=1024):
    m = proj.shape[0]
    nt = seq // rows
    hq = RET_HEADS * RET_QK_DIM // RET_QK_DIM
    hv = 2 * RET_HEADS * RET_QK_DIM // RET_V_DIM
    row = lambda b, h, t: b * nt + t
    in_specs = [
        pl.BlockSpec((1, 1, LANES), lambda b, h, t: (h, 0, 0)),
        pl.BlockSpec((rows, RET_QK_DIM), lambda b, h, t: (row(b, h, t), h)),
        pl.BlockSpec((rows, RET_QK_DIM), lambda b, h, t: (row(b, h, t), hq + h)),
        pl.BlockSpec((rows, RET_V_DIM), lambda b, h, t: (row(b, h, t), hv + h)),
        pl.BlockSpec((rows, RET_V_DIM), lambda b, h, t: (row(b, h, t), hv + RET_HEADS + h)),
        pl.BlockSpec((rows, LANES), lambda b, h, t: (row(b, h, t), 0)),
        pl.BlockSpec((rows, LANES), lambda b, h, t: (row(b, h, t), 0)),
    ]
    blk = (2 * _nbytes((rows, RET_QK_DIM), BF16) + 3 * _nbytes((rows, RET_V_DIM), BF16)
           + 2 * _nbytes((rows, LANES), F32))
    return pl.pallas_call(
        functools.partial(_retention_kernel, rows=rows),
        grid=(batch, RET_HEADS, nt),
        in_specs=in_specs,
        out_specs=pl.BlockSpec((rows, RET_V_DIM), lambda b, h, t: (row(b, h, t), h)),
        out_shape=jax.ShapeDtypeStruct((m, RET_HEADS * RET_V_DIM), BF16),
        scratch_shapes=[pltpu.VMEM((RET_QK_DIM, RET_V_DIM), F32)],
        compiler_params=pltpu.CompilerParams(
            dimension_semantics=("parallel", "parallel", "arbitrary"),
            vmem_limit_bytes=_vmem_limit(blk, scratch_bytes=_nbytes((RET_QK_DIM, RET_V_DIM), F32),
                                         temp_bytes=8 * 1024 * 1024)),
        name="retention",
    )(log_gamma, proj, proj, proj, proj, cos, sin)


def _flash_kernel(q_ref, k_ref, v_ref, o_ref, m_ref, l_ref, acc_ref, *, tq, sub):
    qi = pl.program_id(2)
    nsub = tq // sub
    m_ref[...] = jnp.full_like(m_ref, -jnp.inf)
    l_ref[...] = jnp.zeros_like(l_ref)
    acc_ref[...] = jnp.zeros_like(acc_ref)

    def step(si, start, width, diag_offset):
        rows = slice(si * sub, (si + 1) * sub)
        q = q_ref[rows, :]
        k = k_ref[pl.ds(start, width), :]
        v = v_ref[pl.ds(start, width), :]
        s = lax.dot_general(q, k, (((1,), (1,)), ((), ())), preferred_element_type=F32)
        if diag_offset is not None:
            r = lax.broadcasted_iota(jnp.int32, (sub, width), 0)
            c = lax.broadcasted_iota(jnp.int32, (sub, width), 1)
            s = jnp.where(c <= r + diag_offset, s, NEG_BIG)
        m_prev = m_ref[rows, :]
        m_next = jnp.maximum(m_prev, jnp.max(s, axis=-1, keepdims=True))
        alpha = jnp.exp2(m_prev - m_next)
        p = jnp.exp2(s - jnp.concatenate([m_next] * (width // LANES), axis=1))
        l_ref[rows, :] = alpha * l_ref[rows, :] + jnp.sum(p, axis=-1, keepdims=True)
        acc_ref[rows, :] = alpha * acc_ref[rows, :] + jnp.dot(
            p.astype(v.dtype), v, preferred_element_type=F32)
        m_ref[rows, :] = m_next

    def full_chunk(i, carry):
        start = pl.multiple_of(i * tq, tq)
        for si in range(nsub):
            step(si, start, tq, None)
        return carry

    lax.fori_loop(0, qi, full_chunk, 0)

    tile0 = pl.multiple_of(qi * tq, tq)
    for si in range(nsub):
        for kj in range(si + 1):
            step(si, tile0 + kj * sub, sub, 0 if kj == si else None)
    o_ref[...] = (acc_ref[...] / l_ref[...]).astype(o_ref.dtype)


def _flash(q, k, v, *, batch, seq, heads, tq=1024, sub=512):
    m = q.shape[0]
    nq = seq // tq
    blk = (_nbytes((tq, MLA_HEAD_PAD), BF16) + _nbytes((seq, MLA_HEAD_PAD), BF16)
           + _nbytes((seq, MLA_V_DIM), BF16) + _nbytes((tq, MLA_V_DIM), BF16))
    stat = pltpu.VMEM((tq, LANES), F32)
    return pl.pallas_call(
        functools.partial(_flash_kernel, tq=tq, sub=sub),
        grid=(batch, heads, nq),
        in_specs=[
            pl.BlockSpec((tq, MLA_HEAD_PAD), lambda b, h, i: (b * nq + i, h)),
            pl.BlockSpec((seq, MLA_HEAD_PAD), lambda b, h, i: (b, h)),
            pl.BlockSpec((seq, MLA_V_DIM), lambda b, h, i: (b, h)),
        ],
        out_specs=pl.BlockSpec((tq, MLA_V_DIM), lambda b, h, i: (b * nq + i, h)),
        out_shape=jax.ShapeDtypeStruct((m, heads * MLA_V_DIM), BF16),
        scratch_shapes=[stat, stat, pltpu.VMEM((tq, MLA_V_DIM), F32)],
        compiler_params=pltpu.CompilerParams(
            dimension_semantics=("parallel", "parallel", "arbitrary"),
            vmem_limit_bytes=_vmem_limit(blk, scratch_bytes=3 * _nbytes((tq, LANES), F32),
                                         temp_bytes=4 * (tq // sub) * _nbytes((sub, tq), F32))),
        name="flash_attention",
    )(q, k, v)


def _rope_tables(positions, dim):
    inv_freq = 1.0 / (ROPE_BASE ** (jnp.arange(0, dim, 2, dtype=F32) / dim))
    ang = positions.astype(F32)[..., None] * inv_freq
    return jnp.cos(ang), jnp.sin(ang)


def kernel(x, positions, norm_mix, norm_mlp, ret_w_in, ret_w_out, kv_norm_in, mla_w_kv_down,
           mla_kv_norm, mla_w_kv_up, mla_k_nope_norm, mla_k_pe_norm, mla_w_dq, mla_q_norm,
           mla_w_uq, mla_q_nope_norm, mla_q_pe_norm, mla_w_o, mlp_w1, mlp_w2):
    batch, seq, d_model = x.shape
    m = batch * seq
    depth = norm_mix.shape[0]
    n_ret = ret_w_in.shape[0]
    mla_heads = mla_w_o.shape[1] // MLA_V_DIM
    row = lambda g: g.reshape(1, -1).astype(F32)

    rc, rs = _rope_tables(positions, RET_QK_DIM)
    ret_cos, ret_sin = rc.reshape(m, -1), rs.reshape(m, -1)
    mc, ms_ = _rope_tables(positions, MLA_ROPE_DIM)
    mc, ms_ = mc.reshape(m, -1), ms_.reshape(m, -1)
    zpad = jnp.zeros((m, LANES - MLA_ROPE_DIM), F32)
    mla_cos = jnp.concatenate([mc, mc, zpad], axis=1)
    mla_sin = jnp.concatenate([-ms_, ms_, zpad], axis=1)
    log_gamma = jnp.log1p(-jnp.exp2(-5.0 - jnp.arange(RET_HEADS, dtype=F32)))
    log_gamma = jnp.broadcast_to(log_gamma[:, None, None], (RET_HEADS, 1, LANES))

    def pe_gain(g):
        return jnp.concatenate([g.astype(F32), jnp.zeros((LANES - MLA_ROPE_DIM,), F32)])[None, :]

    xf = x.reshape(m, d_model)
    h = _rmsnorm(xf, row(norm_mix[0]))
    h_kv = None

    def mlp(xf, h, layer, next_gains):
        u = _mm_plain(h, mlp_w1[layer].astype(BF16), tm=1024, tn=1024, out_dtype=BF16,
                      epilogue=_ep_relu2, name="mlp_up")
        return _mm_residual(u, mlp_w2[layer].astype(BF16), xf, next_gains, tm=512, tk=2048,
                            name="mlp_down")

    for layer in range(depth):
        if layer < n_ret:
            proj = _mm_plain(h, ret_w_in[layer].astype(BF16), tm=1024, tn=1024, out_dtype=BF16,
                             epilogue=_ep_cast, name="ret_proj")
            r = _retention(proj, ret_cos, ret_sin, log_gamma, batch=batch, seq=seq)
            xf, h = _mm_residual(r, ret_w_out[layer].astype(BF16), xf, [row(norm_mlp[layer])],
                                 tm=512, tk=2048, name="ret_out")
        else:
            j = layer - n_ret
            if j == 0:
                w_down = jnp.pad(mla_w_kv_down, ((0, 0), (0, LANES - MLA_ROPE_DIM))).astype(BF16)
                n_down = w_down.shape[1]
                c_kv, k_pe = _matmul(
                    h_kv, w_down, tm=1024, tn=n_down, tk=d_model, epilogue=_ep_kv_down,
                    extra=[row(mla_kv_norm), pe_gain(mla_k_pe_norm), mla_cos, mla_sin],
                    extra_specs=[_gain_spec(MLA_KV_RANK), _gain_spec(LANES),
                                 _row_spec(1024, LANES), _row_spec(1024, LANES)],
                    out_shapes=[jax.ShapeDtypeStruct((m, MLA_KV_RANK), BF16),
                                jax.ShapeDtypeStruct((m, LANES), BF16)],
                    out_specs=[_row_spec(1024, MLA_KV_RANK), _row_spec(1024, LANES)],
                    name="kv_down")
                hpt = 4
                k_all, v_all = _matmul(
                    c_kv, mla_w_kv_up.astype(BF16), tm=1024,
                    tn=hpt * (MLA_NOPE_DIM + MLA_V_DIM), tk=MLA_KV_RANK,
                    epilogue=functools.partial(_ep_kv_up, heads=hpt),
                    extra=[row(mla_k_nope_norm), k_pe],
                    extra_specs=[_gain_spec(MLA_NOPE_DIM), _row_spec(1024, LANES)],
                    out_shapes=[jax.ShapeDtypeStruct((m, mla_heads * MLA_HEAD_PAD), BF16),
                                jax.ShapeDtypeStruct((m, mla_heads * MLA_V_DIM), BF16)],
                    out_specs=[pl.BlockSpec((1024, hpt * MLA_HEAD_PAD), lambda i, j, k: (i, j)),
                               pl.BlockSpec((1024, hpt * MLA_V_DIM), lambda i, j, k: (i, j))],
                    name="kv_up")
            q_rank = mla_w_dq.shape[2]
            c_q = _matmul(h, mla_w_dq[j].astype(BF16), tm=1024, tn=q_rank, tk=d_model,
                          epilogue=_ep_norm, extra=[row(mla_q_norm[j])],
                          extra_specs=[_gain_spec(q_rank)],
                          out_shapes=[jax.ShapeDtypeStruct((m, q_rank), BF16)],
                          out_specs=[_row_spec(1024, q_rank)], name="q_down")[0]
            qk_dim = MLA_NOPE_DIM + MLA_ROPE_DIM
            w_uq = mla_w_uq[j].reshape(q_rank, mla_heads, qk_dim)
            w_uq = jnp.pad(w_uq, ((0, 0), (0, 0), (0, MLA_HEAD_PAD - qk_dim)))
            w_uq = w_uq.reshape(q_rank, mla_heads * MLA_HEAD_PAD).astype(BF16)
            hpt = 4
            q_all = _matmul(
                c_q, w_uq, tm=1024, tn=hpt * MLA_HEAD_PAD, tk=q_rank,
                epilogue=functools.partial(_ep_q_up, heads=hpt, scale=qk_dim ** -0.5 * LOG2_E),
                extra=[row(mla_q_nope_norm[j]), pe_gain(mla_q_pe_norm[j]), mla_cos, mla_sin],
                extra_specs=[_gain_spec(MLA_NOPE_DIM), _gain_spec(LANES),
                             _row_spec(1024, LANES), _row_spec(1024, LANES)],
                out_shapes=[jax.ShapeDtypeStruct((m, mla_heads * MLA_HEAD_PAD), BF16)],
                out_specs=[pl.BlockSpec((1024, hpt * MLA_HEAD_PAD), lambda i, j, k: (i, j))],
                name="q_up")[0]
            o = _flash(q_all, k_all, v_all, batch=batch, seq=seq, heads=mla_heads)
            xf, h = _mm_residual(o, mla_w_o[j].astype(BF16), xf, [row(norm_mlp[layer])],
                                 tm=512, tk=d_model, name="attn_out")

        if layer + 1 == depth:
            next_gains = []
        elif layer + 1 == n_ret:
            next_gains = [row(norm_mix[layer + 1]), row(kv_norm_in)]
        else:
            next_gains = [row(norm_mix[layer + 1])]
        outs = mlp(xf, h, layer, next_gains)
        xf = outs[0]
        if layer + 1 == n_ret:
            h, h_kv = outs[1], outs[2]
        elif layer + 1 < depth:
            h = outs[1]

    return xf.reshape(batch, seq, d_model)
```

```python
import functools

import jax
import jax.numpy as jnp
from jax import lax
from jax.experimental import pallas as pl
from jax.experimental.pallas import tpu as pltpu

F32 = jnp.float32
BF16 = jnp.bfloat16

RMS_EPS = 1e-6
ROPE_BASE = 10000.0

RET_HEADS = 8
RET_QK_DIM = 256
RET_V_DIM = 512
RET_CHUNK = 128

MLA_NOPE_DIM = 128
MLA_ROPE_DIM = 64
MLA_V_DIM = 128
MLA_KV_RANK = 512
MLA_HEAD_PAD = 256
MLA_V_PAD = 256

LANES = 128
V7X_VMEM_BYTES = 64 * 1024 * 1024
NEG_BIG = -1e30
LOG2_E = 1.4426950408889634


def _vmem_limit(block_bytes, scratch_bytes=0, temp_bytes=0):
    need = 2 * block_bytes + scratch_bytes + temp_bytes
    return int(min(max(need + need // 4, 16 * 1024 * 1024), V7X_VMEM_BYTES - 6 * 1024 * 1024))


def _nbytes(shape, dtype):
    n = 1
    for s in shape:
        n *= s
    return n * jnp.dtype(dtype).itemsize


def _rms(x, width):
    ms = jnp.sum(x * x, axis=-1, keepdims=True) * (1.0 / width)
    return x * lax.rsqrt(ms + RMS_EPS)


def _norm_rope64(slab, gain128, cos128, sin128):
    y = _rms(slab, LANES) * gain128
    return y * cos128 + pltpu.roll(y, MLA_ROPE_DIM, 1) * sin128


def _rmsnorm_kernel(x_ref, g_ref, o_ref):
    x = x_ref[...].astype(F32)
    o_ref[...] = (_rms(x, x.shape[-1]) * g_ref[...]).astype(o_ref.dtype)


def _rmsnorm(x, gain, *, tm=512):
    m, d = x.shape
    return pl.pallas_call(
        _rmsnorm_kernel,
        grid=(m // tm,),
        in_specs=[pl.BlockSpec((tm, d), lambda i: (i, 0)), pl.BlockSpec((1, d), lambda i: (0, 0))],
        out_specs=pl.BlockSpec((tm, d), lambda i: (i, 0)),
        out_shape=jax.ShapeDtypeStruct((m, d), BF16),
        compiler_params=pltpu.CompilerParams(
            dimension_semantics=("parallel",),
            vmem_limit_bytes=_vmem_limit(_nbytes((tm, d), x.dtype) + _nbytes((tm, d), BF16),
                                         temp_bytes=2 * _nbytes((tm, d), F32))),
        name="rmsnorm",
    )(x, gain)


def _mm_kernel(*refs, nk, n_extra, n_out, epilogue):
    a_ref, w_ref = refs[0], refs[1]
    extra = refs[2:2 + n_extra]
    outs = refs[2 + n_extra:2 + n_extra + n_out]
    if nk == 1:
        acc = jnp.dot(a_ref[...], w_ref[...], preferred_element_type=F32)
        epilogue(acc, extra, outs)
        return
    acc_ref = refs[2 + n_extra + n_out]
    k = pl.program_id(2)

    @pl.when(k == 0)
    def _():
        acc_ref[...] = jnp.zeros_like(acc_ref)

    acc_ref[...] += jnp.dot(a_ref[...], w_ref[...], preferred_element_type=F32)

    @pl.when(k == nk - 1)
    def _():
        epilogue(acc_ref[...], extra, outs)


def _matmul(a, w, *, tm, tn, tk, epilogue, extra=(), extra_specs=(), out_shapes, out_specs, name):
    m, kdim = a.shape
    _, n = w.shape
    nk = kdim // tk
    grid = (m // tm, n // tn, nk)
    in_specs = [pl.BlockSpec((tm, tk), lambda i, j, k: (i, k)),
                pl.BlockSpec((tk, tn), lambda i, j, k: (k, j))] + list(extra_specs)
    scratch = [pltpu.VMEM((tm, tn), F32)] if nk > 1 else []
    blk = _nbytes((tm, tk), a.dtype) + _nbytes((tk, tn), w.dtype)
    for arr, spec in zip(extra, extra_specs):
        blk += _nbytes(spec.block_shape, arr.dtype)
    for sh, spec in zip(out_shapes, out_specs):
        blk += _nbytes(spec.block_shape, sh.dtype)
    acc_bytes = _nbytes((tm, tn), F32)
    kern = functools.partial(_mm_kernel, nk=nk, n_extra=len(extra), n_out=len(out_shapes),
                             epilogue=epilogue)
    return pl.pallas_call(
        kern,
        grid=grid,
        in_specs=in_specs,
        out_specs=list(out_specs),
        out_shape=list(out_shapes),
        scratch_shapes=scratch,
        compiler_params=pltpu.CompilerParams(
            dimension_semantics=("parallel", "parallel", "arbitrary"),
            vmem_limit_bytes=_vmem_limit(blk, scratch_bytes=acc_bytes if nk > 1 else 0,
                                         temp_bytes=2 * acc_bytes)),
        name=name,
    )(a, w, *extra)


def _ep_cast(acc, extra, outs):
    outs[0][...] = acc.astype(outs[0].dtype)


def _ep_relu2(acc, extra, outs):
    r = jnp.maximum(acc, 0.0)
    outs[0][...] = (r * r).astype(outs[0].dtype)


def _ep_norm(acc, extra, outs):
    outs[0][...] = (_rms(acc, acc.shape[-1]) * extra[0][...]).astype(outs[0].dtype)


def _ep_residual(acc, extra, outs):
    x_new = extra[0][...] + acc
    outs[0][...] = x_new
    if len(outs) > 1:
        y = _rms(x_new, x_new.shape[-1])
        for g_ref, o_ref in zip(extra[1:], outs[1:]):
            o_ref[...] = (y * g_ref[...]).astype(o_ref.dtype)


def _ep_kv_down(acc, extra, outs):
    g_kv, g_pe, cos_ref, sin_ref = extra
    c = acc[:, :MLA_KV_RANK]
    outs[0][...] = (_rms(c, MLA_KV_RANK) * g_kv[...]).astype(outs[0].dtype)
    pe = acc[:, MLA_KV_RANK:MLA_KV_RANK + LANES]
    outs[1][...] = _norm_rope64(pe, g_pe[...], cos_ref[...], sin_ref[...]).astype(outs[1].dtype)


def _ep_kv_up(acc, extra, outs, *, heads):
    g_nope, kpe_ref = extra
    k_out, v_out = outs
    kpe = kpe_ref[...]
    for h in range(heads):
        base = h * (MLA_NOPE_DIM + MLA_V_DIM)
        kn = acc[:, base:base + MLA_NOPE_DIM]
        k_out[:, h * MLA_HEAD_PAD:h * MLA_HEAD_PAD + MLA_NOPE_DIM] = (
            _rms(kn, MLA_NOPE_DIM) * g_nope[...]).astype(k_out.dtype)
        k_out[:, h * MLA_HEAD_PAD + MLA_NOPE_DIM:(h + 1) * MLA_HEAD_PAD] = kpe
        v_out[:, h * MLA_V_PAD:h * MLA_V_PAD + MLA_V_DIM] = (
            acc[:, base + MLA_NOPE_DIM:base + MLA_NOPE_DIM + MLA_V_DIM]).astype(v_out.dtype)
        v_out[:, h * MLA_V_PAD + MLA_V_DIM:(h + 1) * MLA_V_PAD] = jnp.ones(
            (acc.shape[0], MLA_V_PAD - MLA_V_DIM), v_out.dtype)


def _q_up_kernel(a_ref, w_ref, gn_ref, gp_ref, cos_ref, sin_ref, q_out, *, heads, scale):
    a = a_ref[...]
    g_nope = gn_ref[...] * scale
    g_pe = gp_ref[...] * scale
    cos128, sin128 = cos_ref[...], sin_ref[...]
    for h in range(heads):
        base = h * MLA_HEAD_PAD
        acc = jnp.dot(a, w_ref[:, base:base + MLA_HEAD_PAD], preferred_element_type=F32)
        q_out[:, base:base + MLA_NOPE_DIM] = (
            _rms(acc[:, :MLA_NOPE_DIM], MLA_NOPE_DIM) * g_nope).astype(q_out.dtype)
        q_out[:, base + MLA_NOPE_DIM:base + MLA_HEAD_PAD] = _norm_rope64(
            acc[:, MLA_NOPE_DIM:], g_pe, cos128, sin128).astype(q_out.dtype)


def _q_up(c_q, w_uq, g_nope, g_pe, cos128, sin128, *, scale, tm=1024, heads_per_tile=4):
    m, kdim = c_q.shape
    n = w_uq.shape[1]
    tn = heads_per_tile * MLA_HEAD_PAD
    blk = (_nbytes((tm, kdim), BF16) + _nbytes((kdim, tn), BF16) + 2 * _nbytes((tm, LANES), F32)
           + _nbytes((tm, tn), BF16))
    return pl.pallas_call(
        functools.partial(_q_up_kernel, heads=heads_per_tile, scale=scale),
        grid=(m // tm, n // tn),
        in_specs=[pl.BlockSpec((tm, kdim), lambda i, j: (i, 0)),
                  pl.BlockSpec((kdim, tn), lambda i, j: (0, j)),
                  pl.BlockSpec((1, MLA_NOPE_DIM), lambda i, j: (0, 0)),
                  pl.BlockSpec((1, LANES), lambda i, j: (0, 0)),
                  pl.BlockSpec((tm, LANES), lambda i, j: (i, 0)),
                  pl.BlockSpec((tm, LANES), lambda i, j: (i, 0))],
        out_specs=pl.BlockSpec((tm, tn), lambda i, j: (i, j)),
        out_shape=jax.ShapeDtypeStruct((m, n), BF16),
        compiler_params=pltpu.CompilerParams(
            dimension_semantics=("parallel", "parallel"),
            vmem_limit_bytes=_vmem_limit(blk, temp_bytes=4 * _nbytes((tm, MLA_HEAD_PAD), F32))),
        name="q_up",
    )(c_q, w_uq, g_nope, g_pe, cos128, sin128)


def _mm_wcast_kernel(a_ref, w_ref, o_ref, w_bf_ref, *, epilogue):
    @pl.when(pl.program_id(1) == 0)
    def _():
        w_bf_ref[...] = w_ref[...].astype(w_bf_ref.dtype)

    acc = jnp.dot(a_ref[...], w_bf_ref[...], preferred_element_type=F32)
    epilogue(acc, (), (o_ref,))


def _mm_wcast(a, w_stack, layer, *, tm, tn, out_dtype, epilogue, name):
    m, kdim = a.shape
    n = w_stack.shape[2]
    blk = (_nbytes((tm, kdim), a.dtype) + _nbytes((kdim, tn), w_stack.dtype)
           + _nbytes((tm, tn), out_dtype))
    return pl.pallas_call(
        functools.partial(_mm_wcast_kernel, epilogue=epilogue),
        grid=(n // tn, m // tm),
        in_specs=[pl.BlockSpec((tm, kdim), lambda j, i: (i, 0)),
                  pl.BlockSpec((None, kdim, tn), lambda j, i: (layer, 0, j))],
        out_specs=pl.BlockSpec((tm, tn), lambda j, i: (i, j)),
        out_shape=jax.ShapeDtypeStruct((m, n), out_dtype),
        scratch_shapes=[pltpu.VMEM((kdim, tn), BF16)],
        compiler_params=pltpu.CompilerParams(
            dimension_semantics=("arbitrary", "arbitrary"),
            vmem_limit_bytes=_vmem_limit(blk, scratch_bytes=_nbytes((kdim, tn), BF16),
                                         temp_bytes=2 * _nbytes((tm, tn), F32))),
        name=name,
    )(a, w_stack)


def _row_spec(tm, width):
    return pl.BlockSpec((tm, width), lambda i, j, k: (i, 0))


def _gain_spec(width):
    return pl.BlockSpec((1, width), lambda i, j, k: (0, 0))


def _mm_residual(a, w, res, gains, *, tm, tk, name):
    m, n = a.shape[0], w.shape[1]
    outs = [jax.ShapeDtypeStruct((m, n), F32)] + [jax.ShapeDtypeStruct((m, n), BF16)] * len(gains)
    return _matmul(a, w, tm=tm, tn=n, tk=tk, epilogue=_ep_residual,
                   extra=[res] + list(gains),
                   extra_specs=[_row_spec(tm, n)] + [_gain_spec(n)] * len(gains),
                   out_shapes=outs, out_specs=[_row_spec(tm, n)] * len(outs), name=name)


def _retention_kernel(lg_ref, q_ref, k_ref, v_ref, g_ref, cos_ref, sin_ref, o_ref, state_ref, *,
                      rows):
    c = RET_CHUNK
    half = RET_QK_DIM // 2

    @pl.when(pl.program_id(2) == 0)
    def _():
        state_ref[...] = jnp.zeros_like(state_ref)

    lg = lg_ref[0]
    n_row = lax.broadcasted_iota(jnp.int32, (c, c), 0).astype(F32)
    n_col = lax.broadcasted_iota(jnp.int32, (c, c), 1).astype(F32)
    diff = n_row - n_col
    decay = jnp.where(diff >= 0, jnp.exp(jnp.maximum(diff, 0.0) * lg), 0.0)
    xi = jnp.exp((n_row + 1.0) * lg)
    zeta = jnp.exp((c - 1.0 - n_row) * lg)
    chunk_decay = jnp.exp(c * lg)[:, :1]
    k_scale = RET_QK_DIM ** -0.5

    for ci in range(rows // c):
        sl = slice(ci * c, (ci + 1) * c)
        cos, sin = cos_ref[sl, :], sin_ref[sl, :]
        q = q_ref[sl, :].astype(F32)
        k = k_ref[sl, :].astype(F32)
        q1, q2 = q[:, :half], q[:, half:]
        k1, k2 = k[:, :half], k[:, half:]
        qr1, qr2 = q1 * cos - q2 * sin, q2 * cos + q1 * sin
        kr1, kr2 = (k1 * cos - k2 * sin) * k_scale, (k2 * cos + k1 * sin) * k_scale
        qr = jnp.concatenate([qr1, qr2], axis=1).astype(BF16)
        kr = jnp.concatenate([kr1, kr2], axis=1).astype(BF16)
        qs = jnp.concatenate([qr1 * xi, qr2 * xi], axis=1).astype(BF16)
        ks = jnp.concatenate([kr1 * zeta, kr2 * zeta], axis=1).astype(BF16)
        v = v_ref[sl, :]

        scores = lax.dot_general(qr, kr, (((1,), (1,)), ((), ())), preferred_element_type=F32)
        scores = scores * decay
        state = state_ref[...]
        o = jnp.dot(scores.astype(BF16), v, preferred_element_type=F32)
        o = o + jnp.dot(qs, state.astype(BF16), preferred_element_type=F32)
        state_ref[...] = chunk_decay * state + lax.dot_general(
            ks, v, (((0,), (0,)), ((), ())), preferred_element_type=F32)

        o = _rms(o, RET_V_DIM)
        gate = g_ref[sl, :].astype(F32)
        o_ref[sl, :] = (gate * jax.nn.sigmoid(gate) * o).astype(o_ref.dtype)


def _retention(proj, cos, sin, log_gamma, *, batch, seq, rows=1024):
    m = proj.shape[0]
    nt = seq // rows
    hq = RET_HEADS * RET_QK_DIM // RET_QK_DIM
    hv = 2 * RET_HEADS * RET_QK_DIM // RET_V_DIM
    row = lambda b, h, t: b * nt + t
    in_specs = [
        pl.BlockSpec((1, 1, LANES), lambda b, h, t: (h, 0, 0)),
        pl.BlockSpec((rows, RET_QK_DIM), lambda b, h, t: (row(b, h, t), h)),
        pl.BlockSpec((rows, RET_QK_DIM), lambda b, h, t: (row(b, h, t), hq + h)),
        pl.BlockSpec((rows, RET_V_DIM), lambda b, h, t: (row(b, h, t), hv + h)),
        pl.BlockSpec((rows, RET_V_DIM), lambda b, h, t: (row(b, h, t), hv + RET_HEADS + h)),
        pl.BlockSpec((rows, LANES), lambda b, h, t: (row(b, h, t), 0)),
        pl.BlockSpec((rows, LANES), lambda b, h, t: (row(b, h, t), 0)),
    ]
    blk = (2 * _nbytes((rows, RET_QK_DIM), BF16) + 3 * _nbytes((rows, RET_V_DIM), BF16)
           + 2 * _nbytes((rows, LANES), F32))
    return pl.pallas_call(
        functools.partial(_retention_kernel, rows=rows),
        grid=(batch, RET_HEADS, nt),
        in_specs=in_specs,
        out_specs=pl.BlockSpec((rows, RET_V_DIM), lambda b, h, t: (row(b, h, t), h)),
        out_shape=jax.ShapeDtypeStruct((m, RET_HEADS * RET_V_DIM), BF16),
        scratch_shapes=[pltpu.VMEM((RET_QK_DIM, RET_V_DIM), F32)],
        compiler_params=pltpu.CompilerParams(
            dimension_semantics=("parallel", "parallel", "arbitrary"),
            vmem_limit_bytes=_vmem_limit(blk, scratch_bytes=_nbytes((RET_QK_DIM, RET_V_DIM), F32),
                                         temp_bytes=8 * 1024 * 1024)),
        name="retention",
    )(log_gamma, proj, proj, proj, proj, cos, sin)


def _flash_kernel(q_ref, k_ref, v_ref, o_ref, m_ref, acc_ref, *, tq, sub, hp, unroll):
    qi = pl.program_id(2)
    nsub = tq // sub
    m_ref[...] = jnp.full_like(m_ref, -jnp.inf)
    acc_ref[...] = jnp.zeros_like(acc_ref)

    def step(hh, si, start, width, diag_offset):
        rows = slice(si * sub, (si + 1) * sub)
        qk_cols = slice(hh * MLA_HEAD_PAD, (hh + 1) * MLA_HEAD_PAD)
        v_cols = slice(hh * MLA_V_PAD, (hh + 1) * MLA_V_PAD)
        q = q_ref[rows, qk_cols]
        k = k_ref[pl.ds(start, width), qk_cols]
        v = v_ref[pl.ds(start, width), v_cols]
        s = lax.dot_general(q, k, (((1,), (1,)), ((), ())), preferred_element_type=F32)
        if diag_offset is not None:
            r = lax.broadcasted_iota(jnp.int32, (sub, width), 0)
            c = lax.broadcasted_iota(jnp.int32, (sub, width), 1)
            s = jnp.where(c <= r + diag_offset, s, NEG_BIG)
        m_prev = m_ref[hh, rows, :]
        m_next = jnp.maximum(m_prev, jnp.max(s, axis=-1, keepdims=True))
        alpha = jnp.exp2(m_prev - m_next)
        p = jnp.exp2(s - jnp.concatenate([m_next] * (width // LANES), axis=1))
        acc_ref[hh, rows, :] = (
            jnp.concatenate([alpha] * (MLA_V_PAD // LANES), axis=1) * acc_ref[hh, rows, :]
            + jnp.dot(p.astype(v.dtype), v, preferred_element_type=F32))
        m_ref[hh, rows, :] = m_next

    def full_chunks(first, count):
        for u in range(count):
            start = pl.multiple_of((first + u) * tq, tq)
            for hh in range(hp):
                for si in range(nsub):
                    step(hh, si, start, tq, None)

    def chunk_group(i, carry):
        full_chunks(i * unroll, unroll)
        return carry

    lax.fori_loop(0, qi // unroll, chunk_group, 0)

    def tail(rem):
        full_chunks(qi - rem, rem)
        tile0 = pl.multiple_of(qi * tq, tq)
        for hh in range(hp):
            for si in range(nsub):
                for kj in range(si + 1):
                    step(hh, si, tile0 + kj * sub, sub, 0 if kj == si else None)
        for hh in range(hp):
            acc = acc_ref[hh]
            o_ref[:, hh * MLA_V_DIM:(hh + 1) * MLA_V_DIM] = (
                acc[:, :MLA_V_DIM] / acc[:, MLA_V_DIM:]).astype(o_ref.dtype)

    for rem in range(unroll):
        pl.when(qi % unroll == rem)(functools.partial(tail, rem))


def _flash(q, k, v, *, batch, seq, heads, tq=1024, sub=512, hp=2, unroll=2):
    m = q.shape[0]
    nq = seq // tq
    blk = (_nbytes((tq, hp * MLA_HEAD_PAD), BF16) + _nbytes((seq, hp * MLA_HEAD_PAD), BF16)
           + _nbytes((seq, hp * MLA_V_PAD), BF16) + _nbytes((tq, hp * MLA_V_DIM), BF16))
    scratch = [pltpu.VMEM((hp, tq, LANES), F32), pltpu.VMEM((hp, tq, MLA_V_PAD), F32)]
    return pl.pallas_call(
        functools.partial(_flash_kernel, tq=tq, sub=sub, hp=hp, unroll=unroll),
        grid=(batch, heads // hp, nq),
        in_specs=[
            pl.BlockSpec((tq, hp * MLA_HEAD_PAD), lambda b, h, i: (b * nq + i, h)),
            pl.BlockSpec((seq, hp * MLA_HEAD_PAD), lambda b, h, i: (b, h)),
            pl.BlockSpec((seq, hp * MLA_V_PAD), lambda b, h, i: (b, h)),
        ],
        out_specs=pl.BlockSpec((tq, hp * MLA_V_DIM), lambda b, h, i: (b * nq + i, h)),
        out_shape=jax.ShapeDtypeStruct((m, heads * MLA_V_DIM), BF16),
        scratch_shapes=scratch,
        compiler_params=pltpu.CompilerParams(
            dimension_semantics=("parallel", "parallel", "arbitrary"),
            vmem_limit_bytes=_vmem_limit(
                blk, scratch_bytes=hp * _nbytes((tq, LANES + MLA_V_PAD), F32),
                temp_bytes=hp * (tq // sub) * _nbytes((sub, tq), F32))),
        name="flash_attention",
    )(q, k, v)


def _rope_tables(positions, dim):
    inv_freq = 1.0 / (ROPE_BASE ** (jnp.arange(0, dim, 2, dtype=F32) / dim))
    ang = positions.astype(F32)[..., None] * inv_freq
    return jnp.cos(ang), jnp.sin(ang)


def kernel(x, positions, norm_mix, norm_mlp, ret_w_in, ret_w_out, kv_norm_in, mla_w_kv_down,
           mla_kv_norm, mla_w_kv_up, mla_k_nope_norm, mla_k_pe_norm, mla_w_dq, mla_q_norm,
           mla_w_uq, mla_q_nope_norm, mla_q_pe_norm, mla_w_o, mlp_w1, mlp_w2):
    batch, seq, d_model = x.shape
    m = batch * seq
    depth = norm_mix.shape[0]
    n_ret = ret_w_in.shape[0]
    mla_heads = mla_w_o.shape[1] // MLA_V_DIM
    row = lambda g: g.reshape(1, -1).astype(F32)

    rc, rs = _rope_tables(positions, RET_QK_DIM)
    ret_cos, ret_sin = rc.reshape(m, -1), rs.reshape(m, -1)
    mc, ms_ = _rope_tables(positions, MLA_ROPE_DIM)
    mc, ms_ = mc.reshape(m, -1), ms_.reshape(m, -1)
    zpad = jnp.zeros((m, LANES - MLA_ROPE_DIM), F32)
    mla_cos = jnp.concatenate([mc, mc, zpad], axis=1)
    mla_sin = jnp.concatenate([-ms_, ms_, zpad], axis=1)
    log_gamma = jnp.log1p(-jnp.exp2(-5.0 - jnp.arange(RET_HEADS, dtype=F32)))
    log_gamma = jnp.broadcast_to(log_gamma[:, None, None], (RET_HEADS, 1, LANES))

    def swap_halves(t):
        half = MLA_ROPE_DIM // 2
        return jnp.concatenate([t[..., half:], t[..., :half]], axis=-1)

    def pe_gain(g):
        g = g.astype(F32)
        return jnp.concatenate([g, swap_halves(g)])[None, :]

    xf = x.reshape(m, d_model)
    h = _rmsnorm(xf, row(norm_mix[0]))
    h_kv = None

    def mlp(xf, h, layer, next_gains):
        u = _mm_wcast(h, mlp_w1, layer, tm=1024, tn=1024, out_dtype=BF16,
                      epilogue=_ep_relu2, name="mlp_up")
        return _mm_residual(u, mlp_w2[layer].astype(BF16), xf, next_gains, tm=512, tk=2048,
                            name="mlp_down")

    for layer in range(depth):
        if layer < n_ret:
            proj = _mm_wcast(h, ret_w_in, layer, tm=1024, tn=1024, out_dtype=BF16,
                             epilogue=_ep_cast, name="ret_proj")
            r = _retention(proj, ret_cos, ret_sin, log_gamma, batch=batch, seq=seq)
            xf, h = _mm_residual(r, ret_w_out[layer].astype(BF16), xf, [row(norm_mlp[layer])],
                                 tm=512, tk=2048, name="ret_out")
        else:
            j = layer - n_ret
            if j == 0:
                w_down = jnp.concatenate(
                    [mla_w_kv_down, swap_halves(mla_w_kv_down[:, MLA_KV_RANK:])], axis=1).astype(BF16)
                n_down = w_down.shape[1]
                c_kv, k_pe = _matmul(
                    h_kv, w_down, tm=1024, tn=n_down, tk=d_model, epilogue=_ep_kv_down,
                    extra=[row(mla_kv_norm), pe_gain(mla_k_pe_norm), mla_cos, mla_sin],
                    extra_specs=[_gain_spec(MLA_KV_RANK), _gain_spec(LANES),
                                 _row_spec(1024, LANES), _row_spec(1024, LANES)],
                    out_shapes=[jax.ShapeDtypeStruct((m, MLA_KV_RANK), BF16),
                                jax.ShapeDtypeStruct((m, LANES), BF16)],
                    out_specs=[_row_spec(1024, MLA_KV_RANK), _row_spec(1024, LANES)],
                    name="kv_down")
                hpt = 4
                k_all, v_all = _matmul(
                    c_kv, mla_w_kv_up.astype(BF16), tm=1024,
                    tn=hpt * (MLA_NOPE_DIM + MLA_V_DIM), tk=MLA_KV_RANK,
                    epilogue=functools.partial(_ep_kv_up, heads=hpt),
                    extra=[row(mla_k_nope_norm), k_pe],
                    extra_specs=[_gain_spec(MLA_NOPE_DIM), _row_spec(1024, LANES)],
                    out_shapes=[jax.ShapeDtypeStruct((m, mla_heads * MLA_HEAD_PAD), BF16),
                                jax.ShapeDtypeStruct((m, mla_heads * MLA_V_PAD), BF16)],
                    out_specs=[pl.BlockSpec((1024, hpt * MLA_HEAD_PAD), lambda i, j, k: (i, j)),
                               pl.BlockSpec((1024, hpt * MLA_V_PAD), lambda i, j, k: (i, j))],
                    name="kv_up")
            q_rank = mla_w_dq.shape[2]
            c_q = _matmul(h, mla_w_dq[j].astype(BF16), tm=1024, tn=q_rank, tk=d_model,
                          epilogue=_ep_norm, extra=[row(mla_q_norm[j])],
                          extra_specs=[_gain_spec(q_rank)],
                          out_shapes=[jax.ShapeDtypeStruct((m, q_rank), BF16)],
                          out_specs=[_row_spec(1024, q_rank)], name="q_down")[0]
            qk_dim = MLA_NOPE_DIM + MLA_ROPE_DIM
            w_uq = mla_w_uq[j].reshape(q_rank, mla_heads, qk_dim)
            w_uq = jnp.concatenate([w_uq, swap_halves(w_uq[..., MLA_NOPE_DIM:])], axis=-1)
            w_uq = w_uq.reshape(q_rank, mla_heads * MLA_HEAD_PAD).astype(BF16)
            q_all = _q_up(c_q, w_uq, row(mla_q_nope_norm[j]), pe_gain(mla_q_pe_norm[j]),
                          mla_cos, mla_sin, scale=qk_dim ** -0.5 * LOG2_E)
            o = _flash(q_all, k_all, v_all, batch=batch, seq=seq, heads=mla_heads)
            xf, h = _mm_residual(o, mla_w_o[j].astype(BF16), xf, [row(norm_mlp[layer])],
                                 tm=512, tk=d_model, name="attn_out")

        if layer + 1 == depth:
            next_gains = []
        elif layer + 1 == n_ret:
            next_gains = [row(norm_mix[layer + 1]), row(kv_norm_in)]
        else:
            next_gains = [row(norm_mix[layer + 1])]
        outs = mlp(xf, h, layer, next_gains)
        xf = outs[0]
        if layer + 1 == n_ret:
            h, h_kv = outs[1], outs[2]
        elif layer + 1 < depth:
            h = outs[1]

    return xf.reshape(batch, seq, d_model)
```

```python
import functools

import jax
import jax.numpy as jnp
from jax import lax
from jax.experimental import pallas as pl
from jax.experimental.pallas import tpu as pltpu

F32 = jnp.float32
BF16 = jnp.bfloat16

RMS_EPS = 1e-6
ROPE_BASE = 10000.0

RET_HEADS = 8
RET_QK_DIM = 256
RET_V_DIM = 512
RET_CHUNK = 128

MLA_NOPE_DIM = 128
MLA_ROPE_DIM = 64
MLA_V_DIM = 128
MLA_KV_RANK = 512
MLA_HEAD_PAD = 256
MLA_V_PAD = 256

LANES = 128
V7X_VMEM_BYTES = 64 * 1024 * 1024
NEG_BIG = -1e30
LOG2_E = 1.4426950408889634


def _vmem_limit(block_bytes, scratch_bytes=0, temp_bytes=0):
    need = 2 * block_bytes + scratch_bytes + temp_bytes
    return int(min(max(need + need // 4, 16 * 1024 * 1024), V7X_VMEM_BYTES - 6 * 1024 * 1024))


def _nbytes(shape, dtype):
    n = 1
    for s in shape:
        n *= s
    return n * jnp.dtype(dtype).itemsize


def _rms(x, width):
    ms = jnp.sum(x * x, axis=-1, keepdims=True) * (1.0 / width)
    return x * lax.rsqrt(ms + RMS_EPS)


def _norm_rope64(slab, gain128, cos128, sin128):
    y = _rms(slab, LANES) * gain128
    return y * cos128 + pltpu.roll(y, MLA_ROPE_DIM, 1) * sin128


def _rmsnorm_kernel(x_ref, g_ref, o_ref):
    x = x_ref[...].astype(F32)
    o_ref[...] = (_rms(x, x.shape[-1]) * g_ref[...]).astype(o_ref.dtype)


def _rmsnorm(x, gain, *, tm=512):
    m, d = x.shape
    return pl.pallas_call(
        _rmsnorm_kernel,
        grid=(m // tm,),
        in_specs=[pl.BlockSpec((tm, d), lambda i: (i, 0)), pl.BlockSpec((1, d), lambda i: (0, 0))],
        out_specs=pl.BlockSpec((tm, d), lambda i: (i, 0)),
        out_shape=jax.ShapeDtypeStruct((m, d), BF16),
        compiler_params=pltpu.CompilerParams(
            dimension_semantics=("parallel",),
            vmem_limit_bytes=_vmem_limit(_nbytes((tm, d), x.dtype) + _nbytes((tm, d), BF16),
                                         temp_bytes=2 * _nbytes((tm, d), F32))),
        name="rmsnorm",
    )(x, gain)


def _mm_kernel(*refs, nk, n_extra, n_out, epilogue):
    a_ref, w_ref = refs[0], refs[1]
    extra = refs[2:2 + n_extra]
    outs = refs[2 + n_extra:2 + n_extra + n_out]
    if nk == 1:
        acc = jnp.dot(a_ref[...], w_ref[...], preferred_element_type=F32)
        epilogue(acc, extra, outs)
        return
    acc_ref = refs[2 + n_extra + n_out]
    k = pl.program_id(2)

    @pl.when(k == 0)
    def _():
        acc_ref[...] = jnp.zeros_like(acc_ref)

    acc_ref[...] += jnp.dot(a_ref[...], w_ref[...], preferred_element_type=F32)

    @pl.when(k == nk - 1)
    def _():
        epilogue(acc_ref[...], extra, outs)


def _matmul(a, w, *, tm, tn, tk, epilogue, extra=(), extra_specs=(), out_shapes, out_specs, name,
            layer=None):
    m, kdim = a.shape
    n = w.shape[-1]
    nk = kdim // tk
    grid = (m // tm, n // tn, nk)
    w_whole = nk == 1 and n == tn
    w_mode = pl.Buffered(1) if w_whole else None
    if layer is None:
        w_spec = pl.BlockSpec((tk, tn), lambda i, j, k: (k, j), pipeline_mode=w_mode)
    else:
        w_spec = pl.BlockSpec((None, tk, tn), lambda i, j, k: (layer, k, j), pipeline_mode=w_mode)
    in_specs = [pl.BlockSpec((tm, tk), lambda i, j, k: (i, k)), w_spec] + list(extra_specs)
    scratch = [pltpu.VMEM((tm, tn), F32)] if nk > 1 else []
    blk = _nbytes((tm, tk), a.dtype) + _nbytes((tk, tn), w.dtype) // (2 if w_whole else 1)
    for arr, spec in zip(extra, extra_specs):
        blk += _nbytes(spec.block_shape, arr.dtype)
    for sh, spec in zip(out_shapes, out_specs):
        blk += _nbytes(spec.block_shape, sh.dtype)
    acc_bytes = _nbytes((tm, tn), F32)
    kern = functools.partial(_mm_kernel, nk=nk, n_extra=len(extra), n_out=len(out_shapes),
                             epilogue=epilogue)
    return pl.pallas_call(
        kern,
        grid=grid,
        in_specs=in_specs,
        out_specs=list(out_specs),
        out_shape=list(out_shapes),
        scratch_shapes=scratch,
        compiler_params=pltpu.CompilerParams(
            dimension_semantics=("parallel", "parallel", "arbitrary"),
            vmem_limit_bytes=_vmem_limit(blk, scratch_bytes=acc_bytes if nk > 1 else 0,
                                         temp_bytes=2 * acc_bytes)),
        name=name,
    )(a, w, *extra)


def _ep_cast(acc, extra, outs):
    outs[0][...] = acc.astype(outs[0].dtype)


def _ep_relu2(acc, extra, outs):
    r = jnp.maximum(acc, 0.0)
    outs[0][...] = (r * r).astype(outs[0].dtype)


def _ep_norm(acc, extra, outs):
    outs[0][...] = (_rms(acc, acc.shape[-1]) * extra[0][...]).astype(outs[0].dtype)


def _ep_residual(acc, extra, outs):
    x_new = extra[0][...] + acc
    outs[0][...] = x_new
    if len(outs) > 1:
        y = _rms(x_new, x_new.shape[-1])
        for g_ref, o_ref in zip(extra[1:], outs[1:]):
            o_ref[...] = (y * g_ref[...]).astype(o_ref.dtype)


def _ep_kv_down(acc, extra, outs):
    g_kv, g_pe, cos_ref, sin_ref = extra
    c = acc[:, :MLA_KV_RANK]
    outs[0][...] = (_rms(c, MLA_KV_RANK) * g_kv[...]).astype(outs[0].dtype)
    pe = acc[:, MLA_KV_RANK:MLA_KV_RANK + LANES]
    outs[1][...] = _norm_rope64(pe, g_pe[...], cos_ref[...], sin_ref[...]).astype(outs[1].dtype)


def _ep_kv_up(acc, extra, outs, *, heads):
    g_nope, kpe_ref = extra
    k_out, v_out = outs
    kpe = kpe_ref[...]
    for h in range(heads):
        base = h * (MLA_NOPE_DIM + MLA_V_DIM)
        kn = acc[:, base:base + MLA_NOPE_DIM]
        k_out[:, h * MLA_HEAD_PAD:h * MLA_HEAD_PAD + MLA_NOPE_DIM] = (
            _rms(kn, MLA_NOPE_DIM) * g_nope[...]).astype(k_out.dtype)
        k_out[:, h * MLA_HEAD_PAD + MLA_NOPE_DIM:(h + 1) * MLA_HEAD_PAD] = kpe
        v_out[:, h * MLA_V_PAD:h * MLA_V_PAD + MLA_V_DIM] = (
            acc[:, base + MLA_NOPE_DIM:base + MLA_NOPE_DIM + MLA_V_DIM]).astype(v_out.dtype)
        v_out[:, h * MLA_V_PAD + MLA_V_DIM:(h + 1) * MLA_V_PAD] = jnp.ones(
            (acc.shape[0], MLA_V_PAD - MLA_V_DIM), v_out.dtype)


def _q_up_kernel(a_ref, w_ref, gn_ref, gp_ref, cos_ref, sin_ref, q_out, *, heads, scale):
    a = a_ref[...]
    g_nope = gn_ref[...] * scale
    g_pe = gp_ref[...] * scale
    cos128, sin128 = cos_ref[...], sin_ref[...]
    for h in range(heads):
        base = h * MLA_HEAD_PAD
        acc = jnp.dot(a, w_ref[:, base:base + MLA_HEAD_PAD], preferred_element_type=F32)
        q_out[:, base:base + MLA_NOPE_DIM] = (
            _rms(acc[:, :MLA_NOPE_DIM], MLA_NOPE_DIM) * g_nope).astype(q_out.dtype)
        q_out[:, base + MLA_NOPE_DIM:base + MLA_HEAD_PAD] = _norm_rope64(
            acc[:, MLA_NOPE_DIM:], g_pe, cos128, sin128).astype(q_out.dtype)


def _q_up(c_q, w_uq, g_nope, g_pe, cos128, sin128, *, scale, tm=1024, heads_per_tile=4):
    m, kdim = c_q.shape
    n = w_uq.shape[1]
    tn = heads_per_tile * MLA_HEAD_PAD
    blk = (_nbytes((tm, kdim), BF16) + _nbytes((kdim, tn), BF16) + 2 * _nbytes((tm, LANES), F32)
           + _nbytes((tm, tn), BF16))
    return pl.pallas_call(
        functools.partial(_q_up_kernel, heads=heads_per_tile, scale=scale),
        grid=(m // tm, n // tn),
        in_specs=[pl.BlockSpec((tm, kdim), lambda i, j: (i, 0)),
                  pl.BlockSpec((kdim, tn), lambda i, j: (0, j)),
                  pl.BlockSpec((1, MLA_NOPE_DIM), lambda i, j: (0, 0)),
                  pl.BlockSpec((1, LANES), lambda i, j: (0, 0)),
                  pl.BlockSpec((tm, LANES), lambda i, j: (i, 0)),
                  pl.BlockSpec((tm, LANES), lambda i, j: (i, 0))],
        out_specs=pl.BlockSpec((tm, tn), lambda i, j: (i, j)),
        out_shape=jax.ShapeDtypeStruct((m, n), BF16),
        compiler_params=pltpu.CompilerParams(
            dimension_semantics=("parallel", "parallel"),
            vmem_limit_bytes=_vmem_limit(blk, temp_bytes=4 * _nbytes((tm, MLA_HEAD_PAD), F32))),
        name="q_up",
    )(c_q, w_uq, g_nope, g_pe, cos128, sin128)


def _mm_wcast_kernel(a_ref, w_ref, o_ref, w_bf_ref, *, epilogue):
    @pl.when(pl.program_id(1) == 0)
    def _():
        w_bf_ref[...] = w_ref[...].astype(w_bf_ref.dtype)

    acc = jnp.dot(a_ref[...], w_bf_ref[...], preferred_element_type=F32)
    epilogue(acc, (), (o_ref,))


def _mm_wcast(a, w_stack, layer, *, tm, tn, out_dtype, epilogue, name):
    m, kdim = a.shape
    n = w_stack.shape[2]
    blk = (_nbytes((tm, kdim), a.dtype) + _nbytes((kdim, tn), w_stack.dtype)
           + _nbytes((tm, tn), out_dtype))
    return pl.pallas_call(
        functools.partial(_mm_wcast_kernel, epilogue=epilogue),
        grid=(n // tn, m // tm),
        in_specs=[pl.BlockSpec((tm, kdim), lambda j, i: (i, 0)),
                  pl.BlockSpec((None, kdim, tn), lambda j, i: (layer, 0, j))],
        out_specs=pl.BlockSpec((tm, tn), lambda j, i: (i, j)),
        out_shape=jax.ShapeDtypeStruct((m, n), out_dtype),
        scratch_shapes=[pltpu.VMEM((kdim, tn), BF16)],
        compiler_params=pltpu.CompilerParams(
            dimension_semantics=("arbitrary", "arbitrary"),
            vmem_limit_bytes=_vmem_limit(blk, scratch_bytes=_nbytes((kdim, tn), BF16),
                                         temp_bytes=2 * _nbytes((tm, tn), F32))),
        name=name,
    )(a, w_stack)


def _row_spec(tm, width):
    return pl.BlockSpec((tm, width), lambda i, j, k: (i, 0))


def _gain_spec(width):
    return pl.BlockSpec((1, width), lambda i, j, k: (0, 0))


def _mm_residual(a, w_stack, layer, res, gains, *, tm, name):
    m, (kdim, n) = a.shape[0], w_stack.shape[1:]
    outs = [jax.ShapeDtypeStruct((m, n), F32)] + [jax.ShapeDtypeStruct((m, n), BF16)] * len(gains)
    return _matmul(a, w_stack, layer=layer, tm=tm, tn=n, tk=kdim, epilogue=_ep_residual,
                   extra=[res] + list(gains),
                   extra_specs=[_row_spec(tm, n)] + [_gain_spec(n)] * len(gains),
                   out_shapes=outs, out_specs=[_row_spec(tm, n)] * len(outs), name=name)


def _retention_kernel(lg_ref, q_ref, k_ref, v_ref, g_ref, cos_ref, sin_ref, o_ref, state_ref, *,
                      rows):
    c = RET_CHUNK
    half = RET_QK_DIM // 2

    @pl.when(pl.program_id(2) == 0)
    def _():
        state_ref[...] = jnp.zeros_like(state_ref)

    lg = lg_ref[0]
    n_row = lax.broadcasted_iota(jnp.int32, (c, c), 0).astype(F32)
    n_col = lax.broadcasted_iota(jnp.int32, (c, c), 1).astype(F32)
    diff = n_row - n_col
    decay = jnp.where(diff >= 0, jnp.exp(jnp.maximum(diff, 0.0) * lg), 0.0)
    xi = jnp.exp((n_row + 1.0) * lg)
    zeta = jnp.exp((c - 1.0 - n_row) * lg)
    chunk_decay = jnp.exp(c * lg)[:, :1]
    k_scale = RET_QK_DIM ** -0.5

    for ci in range(rows // c):
        sl = slice(ci * c, (ci + 1) * c)
        cos, sin = cos_ref[sl, :], sin_ref[sl, :]
        q = q_ref[sl, :].astype(F32)
        k = k_ref[sl, :].astype(F32)
        q1, q2 = q[:, :half], q[:, half:]
        k1, k2 = k[:, :half], k[:, half:]
        qr1, qr2 = q1 * cos - q2 * sin, q2 * cos + q1 * sin
        kr1, kr2 = (k1 * cos - k2 * sin) * k_scale, (k2 * cos + k1 * sin) * k_scale
        qr = jnp.concatenate([qr1, qr2], axis=1).astype(BF16)
        kr = jnp.concatenate([kr1, kr2], axis=1).astype(BF16)
        qs = jnp.concatenate([qr1 * xi, qr2 * xi], axis=1).astype(BF16)
        ks = jnp.concatenate([kr1 * zeta, kr2 * zeta], axis=1).astype(BF16)
        v = v_ref[sl, :]

        scores = lax.dot_general(qr, kr, (((1,), (1,)), ((), ())), preferred_element_type=F32)
        scores = scores * decay
        state = state_ref[...]
        o = jnp.dot(scores.astype(BF16), v, preferred_element_type=F32)
        o = o + jnp.dot(qs, state.astype(BF16), preferred_element_type=F32)
        state_ref[...] = chunk_decay * state + lax.dot_general(
            ks, v, (((0,), (0,)), ((), ())), preferred_element_type=F32)

        o = _rms(o, RET_V_DIM)
        gate = g_ref[sl, :].astype(F32)
        o_ref[sl, :] = (gate * jax.nn.sigmoid(gate) * o).astype(o_ref.dtype)


def _retention(proj, cos, sin, log_gamma, *, batch, seq, rows=1024):
    m = proj.shape[0]
    nt = seq // rows
    hq = RET_HEADS * RET_QK_DIM // RET_QK_DIM
    hv = 2 * RET_HEADS * RET_QK_DIM // RET_V_DIM
    row = lambda b, h, t: b * nt + t
    in_specs = [
        pl.BlockSpec((1, 1, LANES), lambda b, h, t: (h, 0, 0)),
        pl.BlockSpec((rows, RET_QK_DIM), lambda b, h, t: (row(b, h, t), h)),
        pl.BlockSpec((rows, RET_QK_DIM), lambda b, h, t: (row(b, h, t), hq + h)),
        pl.BlockSpec((rows, RET_V_DIM), lambda b, h, t: (row(b, h, t), hv + h)),
        pl.BlockSpec((rows, RET_V_DIM), lambda b, h, t: (row(b, h, t), hv + RET_HEADS + h)),
        pl.BlockSpec((rows, LANES), lambda b, h, t: (row(b, h, t), 0)),
        pl.BlockSpec((rows, LANES), lambda b, h, t: (row(b, h, t), 0)),
    ]
    blk = (2 * _nbytes((rows, RET_QK_DIM), BF16) + 3 * _nbytes((rows, RET_V_DIM), BF16)
           + 2 * _nbytes((rows, LANES), F32))
    return pl.pallas_call(
        functools.partial(_retention_kernel, rows=rows),
        grid=(batch, RET_HEADS, nt),
        in_specs=in_specs,
        out_specs=pl.BlockSpec((rows, RET_V_DIM), lambda b, h, t: (row(b, h, t), h)),
        out_shape=jax.ShapeDtypeStruct((m, RET_HEADS * RET_V_DIM), BF16),
        scratch_shapes=[pltpu.VMEM((RET_QK_DIM, RET_V_DIM), F32)],
        compiler_params=pltpu.CompilerParams(
            dimension_semantics=("parallel", "parallel", "arbitrary"),
            vmem_limit_bytes=_vmem_limit(blk, scratch_bytes=_nbytes((RET_QK_DIM, RET_V_DIM), F32),
                                         temp_bytes=8 * 1024 * 1024)),
        name="retention",
    )(log_gamma, proj, proj, proj, proj, cos, sin)


def _flash_kernel(q_ref, k_ref, v_ref, o_ref, m_ref, acc_ref, *, tq, sub, hp, unroll):
    qi = pl.program_id(2)
    nsub = tq // sub
    m_ref[...] = jnp.full_like(m_ref, -jnp.inf)
    acc_ref[...] = jnp.zeros_like(acc_ref)

    def step(hh, si, start, width, diag_offset):
        rows = slice(si * sub, (si + 1) * sub)
        qk_cols = slice(hh * MLA_HEAD_PAD, (hh + 1) * MLA_HEAD_PAD)
        v_cols = slice(hh * MLA_V_PAD, (hh + 1) * MLA_V_PAD)
        q = q_ref[rows, qk_cols]
        k = k_ref[pl.ds(start, width), qk_cols]
        v = v_ref[pl.ds(start, width), v_cols]
        s = lax.dot_general(q, k, (((1,), (1,)), ((), ())), preferred_element_type=F32)
        if diag_offset is not None:
            r = lax.broadcasted_iota(jnp.int32, (sub, width), 0)
            c = lax.broadcasted_iota(jnp.int32, (sub, width), 1)
            s = jnp.where(c <= r + diag_offset, s, NEG_BIG)
        m_prev = m_ref[hh, rows, :]
        m_next = jnp.maximum(m_prev, jnp.max(s, axis=-1, keepdims=True))
        alpha = jnp.exp2(m_prev - m_next)
        p = jnp.exp2(s - jnp.concatenate([m_next] * (width // LANES), axis=1))
        acc_ref[hh, rows, :] = (
            jnp.concatenate([alpha] * (MLA_V_PAD // LANES), axis=1) * acc_ref[hh, rows, :]
            + jnp.dot(p.astype(v.dtype), v, preferred_element_type=F32))
        m_ref[hh, rows, :] = m_next

    def full_chunks(first, count):
        for u in range(count):
            start = pl.multiple_of((first + u) * tq, tq)
            for hh in range(hp):
                for si in range(nsub):
                    step(hh, si, start, tq, None)

    def chunk_group(i, carry):
        full_chunks(i * unroll, unroll)
        return carry

    lax.fori_loop(0, qi // unroll, chunk_group, 0)

    def tail(rem):
        full_chunks(qi - rem, rem)
        tile0 = pl.multiple_of(qi * tq, tq)
        for hh in range(hp):
            for si in range(nsub):
                for kj in range(si + 1):
                    step(hh, si, tile0 + kj * sub, sub, 0 if kj == si else None)
        for hh in range(hp):
            acc = acc_ref[hh]
            o_ref[:, hh * MLA_V_DIM:(hh + 1) * MLA_V_DIM] = (
                acc[:, :MLA_V_DIM] / acc[:, MLA_V_DIM:]).astype(o_ref.dtype)

    for rem in range(unroll):
        pl.when(qi % unroll == rem)(functools.partial(tail, rem))


def _flash(q, k, v, *, batch, seq, heads, tq=1024, sub=512, hp=2, unroll=2):
    m = q.shape[0]
    nq = seq // tq
    blk = (_nbytes((tq, hp * MLA_HEAD_PAD), BF16) + _nbytes((seq, hp * MLA_HEAD_PAD), BF16)
           + _nbytes((seq, hp * MLA_V_PAD), BF16) + _nbytes((tq, hp * MLA_V_DIM), BF16))
    scratch = [pltpu.VMEM((hp, tq, LANES), F32), pltpu.VMEM((hp, tq, MLA_V_PAD), F32)]
    return pl.pallas_call(
        functools.partial(_flash_kernel, tq=tq, sub=sub, hp=hp, unroll=unroll),
        grid=(batch, heads // hp, nq),
        in_specs=[
            pl.BlockSpec((tq, hp * MLA_HEAD_PAD), lambda b, h, i: (b * nq + i, h)),
            pl.BlockSpec((seq, hp * MLA_HEAD_PAD), lambda b, h, i: (b, h)),
            pl.BlockSpec((seq, hp * MLA_V_PAD), lambda b, h, i: (b, h)),
        ],
        out_specs=pl.BlockSpec((tq, hp * MLA_V_DIM), lambda b, h, i: (b * nq + i, h)),
        out_shape=jax.ShapeDtypeStruct((m, heads * MLA_V_DIM), BF16),
        scratch_shapes=scratch,
        compiler_params=pltpu.CompilerParams(
            dimension_semantics=("parallel", "parallel", "arbitrary"),
            vmem_limit_bytes=_vmem_limit(
                blk, scratch_bytes=hp * _nbytes((tq, LANES + MLA_V_PAD), F32),
                temp_bytes=hp * (tq // sub) * _nbytes((sub, tq), F32))),
        name="flash_attention",
    )(q, k, v)


def _rope_tables(positions, dim):
    inv_freq = 1.0 / (ROPE_BASE ** (jnp.arange(0, dim, 2, dtype=F32) / dim))
    ang = positions.astype(F32)[..., None] * inv_freq
    return jnp.cos(ang), jnp.sin(ang)


def kernel(x, positions, norm_mix, norm_mlp, ret_w_in, ret_w_out, kv_norm_in, mla_w_kv_down,
           mla_kv_norm, mla_w_kv_up, mla_k_nope_norm, mla_k_pe_norm, mla_w_dq, mla_q_norm,
           mla_w_uq, mla_q_nope_norm, mla_q_pe_norm, mla_w_o, mlp_w1, mlp_w2):
    batch, seq, d_model = x.shape
    m = batch * seq
    depth = norm_mix.shape[0]
    n_ret = ret_w_in.shape[0]
    mla_heads = mla_w_o.shape[1] // MLA_V_DIM
    row = lambda g: g.reshape(1, -1).astype(F32)

    rc, rs = _rope_tables(positions, RET_QK_DIM)
    ret_cos, ret_sin = rc.reshape(m, -1), rs.reshape(m, -1)
    mc, ms_ = _rope_tables(positions, MLA_ROPE_DIM)
    mc, ms_ = mc.reshape(m, -1), ms_.reshape(m, -1)
    zpad = jnp.zeros((m, LANES - MLA_ROPE_DIM), F32)
    mla_cos = jnp.concatenate([mc, mc, zpad], axis=1)
    mla_sin = jnp.concatenate([-ms_, ms_, zpad], axis=1)
    log_gamma = jnp.log1p(-jnp.exp2(-5.0 - jnp.arange(RET_HEADS, dtype=F32)))
    log_gamma = jnp.broadcast_to(log_gamma[:, None, None], (RET_HEADS, 1, LANES))

    def swap_halves(t):
        half = MLA_ROPE_DIM // 2
        return jnp.concatenate([t[..., half:], t[..., :half]], axis=-1)

    def pe_gain(g):
        g = g.astype(F32)
        return jnp.concatenate([g, swap_halves(g)])[None, :]

    w2_bf, ret_out_bf = mlp_w2.astype(BF16), ret_w_out.astype(BF16)
    w_o_bf, w_dq_bf = mla_w_o.astype(BF16), mla_w_dq.astype(BF16)

    xf = x.reshape(m, d_model)
    h = _rmsnorm(xf, row(norm_mix[0]))
    h_kv = None

    def mlp(xf, h, layer, next_gains):
        u = _mm_wcast(h, mlp_w1, layer, tm=2048, tn=1024, out_dtype=BF16,
                      epilogue=_ep_relu2, name="mlp_up")
        return _mm_residual(u, w2_bf, layer, xf, next_gains, tm=256, name="mlp_down")

    for layer in range(depth):
        if layer < n_ret:
            proj = _mm_wcast(h, ret_w_in, layer, tm=2048, tn=1024, out_dtype=BF16,
                             epilogue=_ep_cast, name="ret_proj")
            r = _retention(proj, ret_cos, ret_sin, log_gamma, batch=batch, seq=seq)
            xf, h = _mm_residual(r, ret_out_bf, layer, xf, [row(norm_mlp[layer])],
                                 tm=512, name="ret_out")
        else:
            j = layer - n_ret
            if j == 0:
                w_down = jnp.concatenate(
                    [mla_w_kv_down, swap_halves(mla_w_kv_down[:, MLA_KV_RANK:])], axis=1).astype(BF16)
                n_down = w_down.shape[1]
                c_kv, k_pe = _matmul(
                    h_kv, w_down, tm=1024, tn=n_down, tk=d_model, epilogue=_ep_kv_down,
                    extra=[row(mla_kv_norm), pe_gain(mla_k_pe_norm), mla_cos, mla_sin],
                    extra_specs=[_gain_spec(MLA_KV_RANK), _gain_spec(LANES),
                                 _row_spec(1024, LANES), _row_spec(1024, LANES)],
                    out_shapes=[jax.ShapeDtypeStruct((m, MLA_KV_RANK), BF16),
                                jax.ShapeDtypeStruct((m, LANES), BF16)],
                    out_specs=[_row_spec(1024, MLA_KV_RANK), _row_spec(1024, LANES)],
                    name="kv_down")
                hpt = 4
                k_all, v_all = _matmul(
                    c_kv, mla_w_kv_up.astype(BF16), tm=1024,
                    tn=hpt * (MLA_NOPE_DIM + MLA_V_DIM), tk=MLA_KV_RANK,
                    epilogue=functools.partial(_ep_kv_up, heads=hpt),
                    extra=[row(mla_k_nope_norm), k_pe],
                    extra_specs=[_gain_spec(MLA_NOPE_DIM), _row_spec(1024, LANES)],
                    out_shapes=[jax.ShapeDtypeStruct((m, mla_heads * MLA_HEAD_PAD), BF16),
                                jax.ShapeDtypeStruct((m, mla_heads * MLA_V_PAD), BF16)],
                    out_specs=[pl.BlockSpec((1024, hpt * MLA_HEAD_PAD), lambda i, j, k: (i, j)),
                               pl.BlockSpec((1024, hpt * MLA_V_PAD), lambda i, j, k: (i, j))],
                    name="kv_up")
            q_rank = mla_w_dq.shape[2]
            c_q = _matmul(h, w_dq_bf, layer=j, tm=1024, tn=q_rank, tk=d_model,
                          epilogue=_ep_norm, extra=[row(mla_q_norm[j])],
                          extra_specs=[_gain_spec(q_rank)],
                          out_shapes=[jax.ShapeDtypeStruct((m, q_rank), BF16)],
                          out_specs=[_row_spec(1024, q_rank)], name="q_down")[0]
            qk_dim = MLA_NOPE_DIM + MLA_ROPE_DIM
            w_uq = mla_w_uq[j].reshape(q_rank, mla_heads, qk_dim)
            w_uq = jnp.concatenate([w_uq, swap_halves(w_uq[..., MLA_NOPE_DIM:])], axis=-1)
            w_uq = w_uq.reshape(q_rank, mla_heads * MLA_HEAD_PAD).astype(BF16)
            q_all = _q_up(c_q, w_uq, row(mla_q_nope_norm[j]), pe_gain(mla_q_pe_norm[j]),
                          mla_cos, mla_sin, scale=qk_dim ** -0.5 * LOG2_E)
            o = _flash(q_all, k_all, v_all, batch=batch, seq=seq, heads=mla_heads)
            xf, h = _mm_residual(o, w_o_bf, j, xf, [row(norm_mlp[layer])], tm=512,
                                 name="attn_out")

        if layer + 1 == depth:
            next_gains = []
        elif layer + 1 == n_ret:
            next_gains = [row(norm_mix[layer + 1]), row(kv_norm_in)]
        else:
            next_gains = [row(norm_mix[layer + 1])]
        outs = mlp(xf, h, layer, next_gains)
        xf = outs[0]
        if layer + 1 == n_ret:
            h, h_kv = outs[1], outs[2]
        elif layer + 1 < depth:
            h = outs[1]

    return xf.reshape(batch, seq, d_model)
```

```python
import functools

import jax
import jax.numpy as jnp
from jax import lax
from jax.experimental import pallas as pl
from jax.experimental.pallas import tpu as pltpu

F32 = jnp.float32
BF16 = jnp.bfloat16

RMS_EPS = 1e-6
ROPE_BASE = 10000.0

RET_HEADS = 8
RET_QK_DIM = 256
RET_V_DIM = 512
RET_CHUNK = 128

MLA_NOPE_DIM = 128
MLA_ROPE_DIM = 64
MLA_V_DIM = 128
MLA_KV_RANK = 512
MLA_HEAD_PAD = 256
MLA_V_PAD = 256

LANES = 128
V7X_VMEM_BYTES = 64 * 1024 * 1024
NEG_BIG = -1e30
LOG2_E = 1.4426950408889634


def _vmem_limit(block_bytes, scratch_bytes=0, temp_bytes=0):
    need = 2 * block_bytes + scratch_bytes + temp_bytes
    return int(min(max(need + need // 4, 16 * 1024 * 1024), V7X_VMEM_BYTES - 6 * 1024 * 1024))


def _nbytes(shape, dtype):
    n = 1
    for s in shape:
        n *= s
    return n * jnp.dtype(dtype).itemsize


def _rms(x, width):
    ms = jnp.sum(x * x, axis=-1, keepdims=True) * (1.0 / width)
    return x * lax.rsqrt(ms + RMS_EPS)


def _norm_rope64(slab, gain128, cos128, sin128):
    y = _rms(slab, LANES) * gain128
    return y * cos128 + pltpu.roll(y, MLA_ROPE_DIM, 1) * sin128


def _rmsnorm_kernel(x_ref, g_ref, o_ref):
    x = x_ref[...].astype(F32)
    o_ref[...] = (_rms(x, x.shape[-1]) * g_ref[...]).astype(o_ref.dtype)


def _rmsnorm(x, gain, *, tm=512):
    m, d = x.shape
    return pl.pallas_call(
        _rmsnorm_kernel,
        grid=(m // tm,),
        in_specs=[pl.BlockSpec((tm, d), lambda i: (i, 0)), pl.BlockSpec((1, d), lambda i: (0, 0))],
        out_specs=pl.BlockSpec((tm, d), lambda i: (i, 0)),
        out_shape=jax.ShapeDtypeStruct((m, d), BF16),
        compiler_params=pltpu.CompilerParams(
            dimension_semantics=("parallel",),
            vmem_limit_bytes=_vmem_limit(_nbytes((tm, d), x.dtype) + _nbytes((tm, d), BF16),
                                         temp_bytes=2 * _nbytes((tm, d), F32))),
        name="rmsnorm",
    )(x, gain)


def _mm_kernel(*refs, nk, n_extra, n_out, epilogue):
    a_ref, w_ref = refs[0], refs[1]
    extra = refs[2:2 + n_extra]
    outs = refs[2 + n_extra:2 + n_extra + n_out]
    if nk == 1:
        acc = jnp.dot(a_ref[...], w_ref[...], preferred_element_type=F32)
        epilogue(acc, extra, outs)
        return
    acc_ref = refs[2 + n_extra + n_out]
    k = pl.program_id(2)

    @pl.when(k == 0)
    def _():
        acc_ref[...] = jnp.zeros_like(acc_ref)

    acc_ref[...] += jnp.dot(a_ref[...], w_ref[...], preferred_element_type=F32)

    @pl.when(k == nk - 1)
    def _():
        epilogue(acc_ref[...], extra, outs)


def _matmul(a, w, *, tm, tn, tk, epilogue, extra=(), extra_specs=(), out_shapes, out_specs, name,
            layer=None):
    m, kdim = a.shape
    n = w.shape[-1]
    nk = kdim // tk
    grid = (m // tm, n // tn, nk)
    w_whole = nk == 1 and n == tn
    w_mode = pl.Buffered(1) if w_whole else None
    if layer is None:
        w_spec = pl.BlockSpec((tk, tn), lambda i, j, k: (k, j), pipeline_mode=w_mode)
    else:
        w_spec = pl.BlockSpec((None, tk, tn), lambda i, j, k: (layer, k, j), pipeline_mode=w_mode)
    in_specs = [pl.BlockSpec((tm, tk), lambda i, j, k: (i, k)), w_spec] + list(extra_specs)
    scratch = [pltpu.VMEM((tm, tn), F32)] if nk > 1 else []
    blk = _nbytes((tm, tk), a.dtype) + _nbytes((tk, tn), w.dtype) // (2 if w_whole else 1)
    for arr, spec in zip(extra, extra_specs):
        blk += _nbytes(spec.block_shape, arr.dtype)
    for sh, spec in zip(out_shapes, out_specs):
        blk += _nbytes(spec.block_shape, sh.dtype)
    acc_bytes = _nbytes((tm, tn), F32)
    kern = functools.partial(_mm_kernel, nk=nk, n_extra=len(extra), n_out=len(out_shapes),
                             epilogue=epilogue)
    return pl.pallas_call(
        kern,
        grid=grid,
        in_specs=in_specs,
        out_specs=list(out_specs),
        out_shape=list(out_shapes),
        scratch_shapes=scratch,
        compiler_params=pltpu.CompilerParams(
            dimension_semantics=("parallel", "parallel", "arbitrary"),
            vmem_limit_bytes=_vmem_limit(blk, scratch_bytes=acc_bytes if nk > 1 else 0,
                                         temp_bytes=2 * acc_bytes)),
        name=name,
    )(a, w, *extra)


def _ep_cast(acc, extra, outs):
    outs[0][...] = acc.astype(outs[0].dtype)


def _ep_relu2(acc, extra, outs):
    r = jnp.maximum(acc, 0.0)
    outs[0][...] = (r * r).astype(outs[0].dtype)


def _ep_norm(acc, extra, outs):
    outs[0][...] = (_rms(acc, acc.shape[-1]) * extra[0][...]).astype(outs[0].dtype)


def _ep_residual(acc, extra, outs):
    x_new = extra[0][...] + acc
    outs[0][...] = x_new
    if len(outs) > 1:
        y = _rms(x_new, x_new.shape[-1])
        for g_ref, o_ref in zip(extra[1:], outs[1:]):
            o_ref[...] = (y * g_ref[...]).astype(o_ref.dtype)


def _ep_kv_down(acc, extra, outs):
    g_kv, g_pe, cos_ref, sin_ref = extra
    c = acc[:, :MLA_KV_RANK]
    outs[0][...] = (_rms(c, MLA_KV_RANK) * g_kv[...]).astype(outs[0].dtype)
    pe = acc[:, MLA_KV_RANK:MLA_KV_RANK + LANES]
    outs[1][...] = _norm_rope64(pe, g_pe[...], cos_ref[...], sin_ref[...]).astype(outs[1].dtype)


def _ep_kv_up(acc, extra, outs, *, heads):
    g_nope, kpe_ref = extra
    k_out, v_out = outs
    kpe = kpe_ref[...]
    for h in range(heads):
        base = h * (MLA_NOPE_DIM + MLA_V_DIM)
        kn = acc[:, base:base + MLA_NOPE_DIM]
        k_out[:, h * MLA_HEAD_PAD:h * MLA_HEAD_PAD + MLA_NOPE_DIM] = (
            _rms(kn, MLA_NOPE_DIM) * g_nope[...]).astype(k_out.dtype)
        k_out[:, h * MLA_HEAD_PAD + MLA_NOPE_DIM:(h + 1) * MLA_HEAD_PAD] = kpe
        v_out[:, h * MLA_V_PAD:h * MLA_V_PAD + MLA_V_DIM] = (
            acc[:, base + MLA_NOPE_DIM:base + MLA_NOPE_DIM + MLA_V_DIM]).astype(v_out.dtype)
        v_out[:, h * MLA_V_PAD + MLA_V_DIM:(h + 1) * MLA_V_PAD] = jnp.ones(
            (acc.shape[0], MLA_V_PAD - MLA_V_DIM), v_out.dtype)


def _q_up_kernel(a_ref, w_ref, gn_ref, gp_ref, cos_ref, sin_ref, q_out, acc0_ref, acc1_ref, *,
                 heads, scale):
    t = pl.program_id(0)

    @pl.when(t == 0)
    def _():
        acc1_ref[...] = jnp.zeros_like(acc1_ref)

    def body(new_ref, prev_ref):
        a = a_ref[...]
        g_nope = gn_ref[...] * scale
        g_pe = gp_ref[...] * scale
        cos128, sin128 = cos_ref[...], sin_ref[...]
        for h in range(heads):
            base = h * MLA_HEAD_PAD
            new_ref[:, base:base + MLA_HEAD_PAD] = jnp.dot(
                a, w_ref[:, base:base + MLA_HEAD_PAD], preferred_element_type=F32)
            prev = prev_ref[:, base:base + MLA_HEAD_PAD]
            q_out[:, base:base + MLA_NOPE_DIM] = (
                _rms(prev[:, :MLA_NOPE_DIM], MLA_NOPE_DIM) * g_nope).astype(q_out.dtype)
            q_out[:, base + MLA_NOPE_DIM:base + MLA_HEAD_PAD] = _norm_rope64(
                prev[:, MLA_NOPE_DIM:], g_pe, cos128, sin128).astype(q_out.dtype)

    parity = lax.rem(t, 2)
    pl.when(parity == 0)(functools.partial(body, acc0_ref, acc1_ref))
    pl.when(parity == 1)(functools.partial(body, acc1_ref, acc0_ref))


def _q_up(c_q, w_uq, g_nope, g_pe, cos128, sin128, *, scale, tm=1024, heads_per_tile=4):
    m, kdim = c_q.shape
    n = w_uq.shape[1]
    tn = heads_per_tile * MLA_HEAD_PAD
    nj = n // tn
    n_tiles = (m // tm) * nj
    cur = lambda t: jnp.minimum(t, n_tiles - 1)
    prv = lambda t: jnp.maximum(t - 1, 0)
    blk = (_nbytes((tm, kdim), BF16) + _nbytes((kdim, tn), BF16) + 2 * _nbytes((tm, LANES), F32)
           + _nbytes((tm, tn), BF16))
    return pl.pallas_call(
        functools.partial(_q_up_kernel, heads=heads_per_tile, scale=scale),
        grid=(n_tiles + 1,),
        in_specs=[pl.BlockSpec((tm, kdim), lambda t: (cur(t) // nj, 0)),
                  pl.BlockSpec((kdim, tn), lambda t: (0, cur(t) % nj)),
                  pl.BlockSpec((1, MLA_NOPE_DIM), lambda t: (0, 0)),
                  pl.BlockSpec((1, LANES), lambda t: (0, 0)),
                  pl.BlockSpec((tm, LANES), lambda t: (prv(t) // nj, 0)),
                  pl.BlockSpec((tm, LANES), lambda t: (prv(t) // nj, 0))],
        out_specs=pl.BlockSpec((tm, tn), lambda t: (prv(t) // nj, prv(t) % nj)),
        out_shape=jax.ShapeDtypeStruct((m, n), BF16),
        scratch_shapes=[pltpu.VMEM((tm, tn), F32), pltpu.VMEM((tm, tn), F32)],
        compiler_params=pltpu.CompilerParams(
            dimension_semantics=("arbitrary",),
            vmem_limit_bytes=_vmem_limit(blk, scratch_bytes=_nbytes((2, tm, tn), F32),
                                         temp_bytes=4 * _nbytes((tm, MLA_HEAD_PAD), F32))),
        name="q_up",
    )(c_q, w_uq, g_nope, g_pe, cos128, sin128)


def _mm_wcast_kernel(a_ref, w_ref, o_ref, w_bf_ref, *, epilogue):
    @pl.when(pl.program_id(1) == 0)
    def _():
        w_bf_ref[...] = w_ref[...].astype(w_bf_ref.dtype)

    acc = jnp.dot(a_ref[...], w_bf_ref[...], preferred_element_type=F32)
    epilogue(acc, (), (o_ref,))


def _mm_wcast(a, w_stack, layer, *, tm, tn, out_dtype, epilogue, name):
    m, kdim = a.shape
    n = w_stack.shape[2]
    blk = (_nbytes((tm, kdim), a.dtype) + _nbytes((kdim, tn), w_stack.dtype)
           + _nbytes((tm, tn), out_dtype))
    return pl.pallas_call(
        functools.partial(_mm_wcast_kernel, epilogue=epilogue),
        grid=(n // tn, m // tm),
        in_specs=[pl.BlockSpec((tm, kdim), lambda j, i: (i, 0)),
                  pl.BlockSpec((None, kdim, tn), lambda j, i: (layer, 0, j))],
        out_specs=pl.BlockSpec((tm, tn), lambda j, i: (i, j)),
        out_shape=jax.ShapeDtypeStruct((m, n), out_dtype),
        scratch_shapes=[pltpu.VMEM((kdim, tn), BF16)],
        compiler_params=pltpu.CompilerParams(
            dimension_semantics=("arbitrary", "arbitrary"),
            vmem_limit_bytes=_vmem_limit(blk, scratch_bytes=_nbytes((kdim, tn), BF16),
                                         temp_bytes=2 * _nbytes((tm, tn), F32))),
        name=name,
    )(a, w_stack)


def _row_spec(tm, width):
    return pl.BlockSpec((tm, width), lambda i, j, k: (i, 0))


def _gain_spec(width):
    return pl.BlockSpec((1, width), lambda i, j, k: (0, 0))


def _mm_residual(a, w_stack, layer, res, gains, *, tm, name):
    m, (kdim, n) = a.shape[0], w_stack.shape[1:]
    outs = [jax.ShapeDtypeStruct((m, n), F32)] + [jax.ShapeDtypeStruct((m, n), BF16)] * len(gains)
    return _matmul(a, w_stack, layer=layer, tm=tm, tn=n, tk=kdim, epilogue=_ep_residual,
                   extra=[res] + list(gains),
                   extra_specs=[_row_spec(tm, n)] + [_gain_spec(n)] * len(gains),
                   out_shapes=outs, out_specs=[_row_spec(tm, n)] * len(outs), name=name)


def _retention_kernel(lg_ref, q_ref, k_ref, v_ref, g_ref, cos_ref, sin_ref, o_ref, state_ref, *,
                      rows):
    c = RET_CHUNK
    half = RET_QK_DIM // 2

    @pl.when(pl.program_id(2) == 0)
    def _():
        state_ref[...] = jnp.zeros_like(state_ref)

    lg = lg_ref[0]
    n_row = lax.broadcasted_iota(jnp.int32, (c, c), 0).astype(F32)
    n_col = lax.broadcasted_iota(jnp.int32, (c, c), 1).astype(F32)
    diff = n_row - n_col
    decay = jnp.where(diff >= 0, jnp.exp(jnp.maximum(diff, 0.0) * lg), 0.0)
    xi = jnp.exp((n_row + 1.0) * lg)
    zeta = jnp.exp((c - 1.0 - n_row) * lg)
    chunk_decay = jnp.exp(c * lg)[:, :1]
    k_scale = RET_QK_DIM ** -0.5

    for ci in range(rows // c):
        sl = slice(ci * c, (ci + 1) * c)
        cos, sin = cos_ref[sl, :], sin_ref[sl, :]
        q = q_ref[sl, :].astype(F32)
        k = k_ref[sl, :].astype(F32)
        q1, q2 = q[:, :half], q[:, half:]
        k1, k2 = k[:, :half], k[:, half:]
        qr1, qr2 = q1 * cos - q2 * sin, q2 * cos + q1 * sin
        kr1, kr2 = (k1 * cos - k2 * sin) * k_scale, (k2 * cos + k1 * sin) * k_scale
        qr = jnp.concatenate([qr1, qr2], axis=1).astype(BF16)
        kr = jnp.concatenate([kr1, kr2], axis=1).astype(BF16)
        qs = jnp.concatenate([qr1 * xi, qr2 * xi], axis=1).astype(BF16)
        ks = jnp.concatenate([kr1 * zeta, kr2 * zeta], axis=1).astype(BF16)
        v = v_ref[sl, :]

        scores = lax.dot_general(qr, kr, (((1,), (1,)), ((), ())), preferred_element_type=F32)
        scores = scores * decay
        state = state_ref[...]
        o = jnp.dot(scores.astype(BF16), v, preferred_element_type=F32)
        o = o + jnp.dot(qs, state.astype(BF16), preferred_element_type=F32)
        state_ref[...] = chunk_decay * state + lax.dot_general(
            ks, v, (((0,), (0,)), ((), ())), preferred_element_type=F32)

        o = _rms(o, RET_V_DIM)
        gate = g_ref[sl, :].astype(F32)
        o_ref[sl, :] = (gate * jax.nn.sigmoid(gate) * o).astype(o_ref.dtype)


def _retention(proj, cos, sin, log_gamma, *, batch, seq, rows=1024):
    m = proj.shape[0]
    nt = seq // rows
    hq = RET_HEADS * RET_QK_DIM // RET_QK_DIM
    hv = 2 * RET_HEADS * RET_QK_DIM // RET_V_DIM
    row = lambda b, h, t: b * nt + t
    in_specs = [
        pl.BlockSpec((1, 1, LANES), lambda b, h, t: (h, 0, 0)),
        pl.BlockSpec((rows, RET_QK_DIM), lambda b, h, t: (row(b, h, t), h)),
        pl.BlockSpec((rows, RET_QK_DIM), lambda b, h, t: (row(b, h, t), hq + h)),
        pl.BlockSpec((rows, RET_V_DIM), lambda b, h, t: (row(b, h, t), hv + h)),
        pl.BlockSpec((rows, RET_V_DIM), lambda b, h, t: (row(b, h, t), hv + RET_HEADS + h)),
        pl.BlockSpec((rows, LANES), lambda b, h, t: (row(b, h, t), 0)),
        pl.BlockSpec((rows, LANES), lambda b, h, t: (row(b, h, t), 0)),
    ]
    blk = (2 * _nbytes((rows, RET_QK_DIM), BF16) + 3 * _nbytes((rows, RET_V_DIM), BF16)
           + 2 * _nbytes((rows, LANES), F32))
    return pl.pallas_call(
        functools.partial(_retention_kernel, rows=rows),
        grid=(batch, RET_HEADS, nt),
        in_specs=in_specs,
        out_specs=pl.BlockSpec((rows, RET_V_DIM), lambda b, h, t: (row(b, h, t), h)),
        out_shape=jax.ShapeDtypeStruct((m, RET_HEADS * RET_V_DIM), BF16),
        scratch_shapes=[pltpu.VMEM((RET_QK_DIM, RET_V_DIM), F32)],
        compiler_params=pltpu.CompilerParams(
            dimension_semantics=("parallel", "parallel", "arbitrary"),
            vmem_limit_bytes=_vmem_limit(blk, scratch_bytes=_nbytes((RET_QK_DIM, RET_V_DIM), F32),
                                         temp_bytes=8 * 1024 * 1024)),
        name="retention",
    )(log_gamma, proj, proj, proj, proj, cos, sin)


def _flash_kernel(q_ref, k_ref, v_ref, o_ref, m_ref, acc_ref, *, tq, sub, kw, hp, unroll, nq):
    qi = pl.program_id(2)
    nsub = tq // sub
    m_ref[...] = jnp.full_like(m_ref, -jnp.inf)
    acc_ref[...] = jnp.zeros_like(acc_ref)

    def step(hh, si, start, width, diag_offset):
        rows = slice(si * sub, (si + 1) * sub)
        qk_cols = slice(hh * MLA_HEAD_PAD, (hh + 1) * MLA_HEAD_PAD)
        v_cols = slice(hh * MLA_V_PAD, (hh + 1) * MLA_V_PAD)
        q = q_ref[rows, qk_cols]
        k = k_ref[pl.ds(start, width), qk_cols]
        v = v_ref[pl.ds(start, width), v_cols]
        s = lax.dot_general(q, k, (((1,), (1,)), ((), ())), preferred_element_type=F32)
        if diag_offset is not None:
            r = lax.broadcasted_iota(jnp.int32, (sub, width), 0)
            c = lax.broadcasted_iota(jnp.int32, (sub, width), 1)
            s = jnp.where(c <= r + diag_offset, s, NEG_BIG)
        m_prev = m_ref[hh, rows, :]
        m_next = jnp.maximum(m_prev, jnp.max(s, axis=-1, keepdims=True))
        alpha = jnp.exp2(m_prev - m_next)
        p = jnp.exp2(s - jnp.concatenate([m_next] * (width // LANES), axis=1))
        acc_ref[hh, rows, :] = (
            jnp.concatenate([alpha] * (MLA_V_PAD // LANES), axis=1) * acc_ref[hh, rows, :]
            + jnp.dot(p.astype(v.dtype), v, preferred_element_type=F32))
        m_ref[hh, rows, :] = m_next

    def full_chunks(first, count):
        for u in range(count):
            start = pl.multiple_of((first + u) * kw, kw)
            for hh in range(hp):
                for si in range(nsub):
                    step(hh, si, start, kw, None)

    def chunk_group(i, carry):
        full_chunks(i * unroll, unroll)
        return carry

    n_full = qi * (tq // kw)
    lax.fori_loop(0, n_full // unroll, chunk_group, 0)

    def tail(rem):
        full_chunks(n_full - rem, rem)
        tile0 = pl.multiple_of(qi * tq, tq)
        for hh in range(hp):
            for si in range(nsub):
                for kj in range(si + 1):
                    step(hh, si, tile0 + kj * sub, sub, 0 if kj == si else None)
        for hh in range(hp):
            acc = acc_ref[hh]
            o_ref[:, hh * MLA_V_DIM:(hh + 1) * MLA_V_DIM] = (
                acc[:, :MLA_V_DIM] / acc[:, MLA_V_DIM:]).astype(o_ref.dtype)

    for rem in sorted({(i * (tq // kw)) % unroll for i in range(nq)}):
        pl.when(n_full % unroll == rem)(functools.partial(tail, rem))


def _flash(q, k, v, *, batch, seq, heads, tq=2048, sub=512, kw=1024, hp=2, unroll=2):
    m = q.shape[0]
    nq = seq // tq
    blk = (_nbytes((tq, hp * MLA_HEAD_PAD), BF16) + _nbytes((seq, hp * MLA_HEAD_PAD), BF16)
           + _nbytes((seq, hp * MLA_V_PAD), BF16) + _nbytes((tq, hp * MLA_V_DIM), BF16))
    scratch = [pltpu.VMEM((hp, tq, LANES), F32), pltpu.VMEM((hp, tq, MLA_V_PAD), F32)]
    return pl.pallas_call(
        functools.partial(_flash_kernel, tq=tq, sub=sub, kw=kw, hp=hp, unroll=unroll, nq=nq),
        grid=(batch, heads // hp, nq),
        in_specs=[
            pl.BlockSpec((tq, hp * MLA_HEAD_PAD), lambda b, h, i: (b * nq + i, h)),
            pl.BlockSpec((seq, hp * MLA_HEAD_PAD), lambda b, h, i: (b, h)),
            pl.BlockSpec((seq, hp * MLA_V_PAD), lambda b, h, i: (b, h)),
        ],
        out_specs=pl.BlockSpec((tq, hp * MLA_V_DIM), lambda b, h, i: (b * nq + i, h)),
        out_shape=jax.ShapeDtypeStruct((m, heads * MLA_V_DIM), BF16),
        scratch_shapes=scratch,
        compiler_params=pltpu.CompilerParams(
            dimension_semantics=("parallel", "parallel", "arbitrary"),
            vmem_limit_bytes=_vmem_limit(
                blk, scratch_bytes=hp * _nbytes((tq, LANES + MLA_V_PAD), F32),
                temp_bytes=hp * (tq // sub) * _nbytes((sub, kw), F32))),
        name="flash_attention",
    )(q, k, v)


def _rope_tables(positions, dim):
    inv_freq = 1.0 / (ROPE_BASE ** (jnp.arange(0, dim, 2, dtype=F32) / dim))
    ang = positions.astype(F32)[..., None] * inv_freq
    return jnp.cos(ang), jnp.sin(ang)


def kernel(x, positions, norm_mix, norm_mlp, ret_w_in, ret_w_out, kv_norm_in, mla_w_kv_down,
           mla_kv_norm, mla_w_kv_up, mla_k_nope_norm, mla_k_pe_norm, mla_w_dq, mla_q_norm,
           mla_w_uq, mla_q_nope_norm, mla_q_pe_norm, mla_w_o, mlp_w1, mlp_w2):
    batch, seq, d_model = x.shape
    m = batch * seq
    depth = norm_mix.shape[0]
    n_ret = ret_w_in.shape[0]
    mla_heads = mla_w_o.shape[1] // MLA_V_DIM
    row = lambda g: g.reshape(1, -1).astype(F32)

    rc, rs = _rope_tables(positions, RET_QK_DIM)
    ret_cos, ret_sin = rc.reshape(m, -1), rs.reshape(m, -1)
    mc, ms_ = _rope_tables(positions, MLA_ROPE_DIM)
    mc, ms_ = mc.reshape(m, -1), ms_.reshape(m, -1)
    zpad = jnp.zeros((m, LANES - MLA_ROPE_DIM), F32)
    mla_cos = jnp.concatenate([mc, mc, zpad], axis=1)
    mla_sin = jnp.concatenate([-ms_, ms_, zpad], axis=1)
    log_gamma = jnp.log1p(-jnp.exp2(-5.0 - jnp.arange(RET_HEADS, dtype=F32)))
    log_gamma = jnp.broadcast_to(log_gamma[:, None, None], (RET_HEADS, 1, LANES))

    def swap_halves(t):
        half = MLA_ROPE_DIM // 2
        return jnp.concatenate([t[..., half:], t[..., :half]], axis=-1)

    def pe_gain(g):
        g = g.astype(F32)
        return jnp.concatenate([g, swap_halves(g)])[None, :]

    w2_bf, ret_out_bf = mlp_w2.astype(BF16), ret_w_out.astype(BF16)
    w_o_bf, w_dq_bf = mla_w_o.astype(BF16), mla_w_dq.astype(BF16)

    xf = x.reshape(m, d_model)
    h = _rmsnorm(xf, row(norm_mix[0]))
    h_kv = None

    def mlp(xf, h, layer, next_gains):
        u = _mm_wcast(h, mlp_w1, layer, tm=2048, tn=1024, out_dtype=BF16,
                      epilogue=_ep_relu2, name="mlp_up")
        return _mm_residual(u, w2_bf, layer, xf, next_gains, tm=256, name="mlp_down")

    for layer in range(depth):
        if layer < n_ret:
            proj = _mm_wcast(h, ret_w_in, layer, tm=2048, tn=1024, out_dtype=BF16,
                             epilogue=_ep_cast, name="ret_proj")
            r = _retention(proj, ret_cos, ret_sin, log_gamma, batch=batch, seq=seq)
            xf, h = _mm_residual(r, ret_out_bf, layer, xf, [row(norm_mlp[layer])],
                                 tm=512, name="ret_out")
        else:
            j = layer - n_ret
            if j == 0:
                w_down = jnp.concatenate(
                    [mla_w_kv_down, swap_halves(mla_w_kv_down[:, MLA_KV_RANK:])], axis=1).astype(BF16)
                n_down = w_down.shape[1]
                c_kv, k_pe = _matmul(
                    h_kv, w_down, tm=1024, tn=n_down, tk=d_model, epilogue=_ep_kv_down,
                    extra=[row(mla_kv_norm), pe_gain(mla_k_pe_norm), mla_cos, mla_sin],
                    extra_specs=[_gain_spec(MLA_KV_RANK), _gain_spec(LANES),
                                 _row_spec(1024, LANES), _row_spec(1024, LANES)],
                    out_shapes=[jax.ShapeDtypeStruct((m, MLA_KV_RANK), BF16),
                                jax.ShapeDtypeStruct((m, LANES), BF16)],
                    out_specs=[_row_spec(1024, MLA_KV_RANK), _row_spec(1024, LANES)],
                    name="kv_down")
                hpt = 4
                k_all, v_all = _matmul(
                    c_kv, mla_w_kv_up.astype(BF16), tm=1024,
                    tn=hpt * (MLA_NOPE_DIM + MLA_V_DIM), tk=MLA_KV_RANK,
                    epilogue=functools.partial(_ep_kv_up, heads=hpt),
                    extra=[row(mla_k_nope_norm), k_pe],
                    extra_specs=[_gain_spec(MLA_NOPE_DIM), _row_spec(1024, LANES)],
                    out_shapes=[jax.ShapeDtypeStruct((m, mla_heads * MLA_HEAD_PAD), BF16),
                                jax.ShapeDtypeStruct((m, mla_heads * MLA_V_PAD), BF16)],
                    out_specs=[pl.BlockSpec((1024, hpt * MLA_HEAD_PAD), lambda i, j, k: (i, j)),
                               pl.BlockSpec((1024, hpt * MLA_V_PAD), lambda i, j, k: (i, j))],
                    name="kv_up")
            q_rank = mla_w_dq.shape[2]
            c_q = _matmul(h, w_dq_bf, layer=j, tm=1024, tn=q_rank, tk=d_model,
                          epilogue=_ep_norm, extra=[row(mla_q_norm[j])],
                          extra_specs=[_gain_spec(q_rank)],
                          out_shapes=[jax.ShapeDtypeStruct((m, q_rank), BF16)],
                          out_specs=[_row_spec(1024, q_rank)], name="q_down")[0]
            qk_dim = MLA_NOPE_DIM + MLA_ROPE_DIM
            w_uq = mla_w_uq[j].reshape(q_rank, mla_heads, qk_dim)
            w_uq = jnp.concatenate([w_uq, swap_halves(w_uq[..., MLA_NOPE_DIM:])], axis=-1)
            w_uq = w_uq.reshape(q_rank, mla_heads * MLA_HEAD_PAD).astype(BF16)
            q_all = _q_up(c_q, w_uq, row(mla_q_nope_norm[j]), pe_gain(mla_q_pe_norm[j]),
                          mla_cos, mla_sin, scale=qk_dim ** -0.5 * LOG2_E)
            o = _flash(q_all, k_all, v_all, batch=batch, seq=seq, heads=mla_heads)
            xf, h = _mm_residual(o, w_o_bf, j, xf, [row(norm_mlp[layer])], tm=512,
                                 name="attn_out")

        if layer + 1 == depth:
            next_gains = []
        elif layer + 1 == n_ret:
            next_gains = [row(norm_mix[layer + 1]), row(kv_norm_in)]
        else:
            next_gains = [row(norm_mix[layer + 1])]
        outs = mlp(xf, h, layer, next_gains)
        xf = outs[0]
        if layer + 1 == n_ret:
            h, h_kv = outs[1], outs[2]
        elif layer + 1 < depth:
            h = outs[1]

    return xf.reshape(batch, seq, d_model)
```

```python
import functools

import jax
import jax.numpy as jnp
from jax import lax
from jax.experimental import pallas as pl
from jax.experimental.pallas import tpu as pltpu

F32 = jnp.float32
BF16 = jnp.bfloat16

RMS_EPS = 1e-6
ROPE_BASE = 10000.0

RET_HEADS = 8
RET_QK_DIM = 256
RET_V_DIM = 512
RET_CHUNK = 128

MLA_NOPE_DIM = 128
MLA_ROPE_DIM = 64
MLA_V_DIM = 128
MLA_KV_RANK = 512
MLA_HEAD_PAD = 256
MLA_V_PAD = 256

LANES = 128
V7X_VMEM_BYTES = 64 * 1024 * 1024
NEG_BIG = -1e30
LOG2_E = 1.4426950408889634


def _vmem_limit(block_bytes, scratch_bytes=0, temp_bytes=0):
    need = 2 * block_bytes + scratch_bytes + temp_bytes
    return int(min(max(need + need // 4, 16 * 1024 * 1024), V7X_VMEM_BYTES - 6 * 1024 * 1024))


def _nbytes(shape, dtype):
    n = 1
    for s in shape:
        n *= s
    return n * jnp.dtype(dtype).itemsize


def _rms(x, width):
    ms = jnp.sum(x * x, axis=-1, keepdims=True) * (1.0 / width)
    return x * lax.rsqrt(ms + RMS_EPS)


def _norm_rope64(slab, gain128, cos128, sin128):
    y = _rms(slab, LANES) * gain128
    return y * cos128 + pltpu.roll(y, MLA_ROPE_DIM, 1) * sin128


def _rmsnorm_kernel(x_ref, g_ref, o_ref):
    x = x_ref[...].astype(F32)
    o_ref[...] = (_rms(x, x.shape[-1]) * g_ref[...]).astype(o_ref.dtype)


def _rmsnorm(x, gain, *, tm=512):
    m, d = x.shape
    return pl.pallas_call(
        _rmsnorm_kernel,
        grid=(m // tm,),
        in_specs=[pl.BlockSpec((tm, d), lambda i: (i, 0)), pl.BlockSpec((1, d), lambda i: (0, 0))],
        out_specs=pl.BlockSpec((tm, d), lambda i: (i, 0)),
        out_shape=jax.ShapeDtypeStruct((m, d), BF16),
        compiler_params=pltpu.CompilerParams(
            dimension_semantics=("parallel",),
            vmem_limit_bytes=_vmem_limit(_nbytes((tm, d), x.dtype) + _nbytes((tm, d), BF16),
                                         temp_bytes=2 * _nbytes((tm, d), F32))),
        name="rmsnorm",
    )(x, gain)


def _mm_kernel(*refs, nk, n_extra, n_out, epilogue):
    a_ref, w_ref = refs[0], refs[1]
    extra = refs[2:2 + n_extra]
    outs = refs[2 + n_extra:2 + n_extra + n_out]
    if nk == 1:
        acc = jnp.dot(a_ref[...], w_ref[...], preferred_element_type=F32)
        epilogue(acc, extra, outs)
        return
    acc_ref = refs[2 + n_extra + n_out]
    k = pl.program_id(2)

    @pl.when(k == 0)
    def _():
        acc_ref[...] = jnp.zeros_like(acc_ref)

    acc_ref[...] += jnp.dot(a_ref[...], w_ref[...], preferred_element_type=F32)

    @pl.when(k == nk - 1)
    def _():
        epilogue(acc_ref[...], extra, outs)


def _matmul(a, w, *, tm, tn, tk, epilogue, extra=(), extra_specs=(), out_shapes, out_specs, name,
            layer=None):
    m, kdim = a.shape
    n = w.shape[-1]
    nk = kdim // tk
    grid = (m // tm, n // tn, nk)
    w_whole = nk == 1 and n == tn
    w_mode = pl.Buffered(1) if w_whole else None
    if layer is None:
        w_spec = pl.BlockSpec((tk, tn), lambda i, j, k: (k, j), pipeline_mode=w_mode)
    else:
        w_spec = pl.BlockSpec((None, tk, tn), lambda i, j, k: (layer, k, j), pipeline_mode=w_mode)
    in_specs = [pl.BlockSpec((tm, tk), lambda i, j, k: (i, k)), w_spec] + list(extra_specs)
    scratch = [pltpu.VMEM((tm, tn), F32)] if nk > 1 else []
    blk = _nbytes((tm, tk), a.dtype) + _nbytes((tk, tn), w.dtype) // (2 if w_whole else 1)
    for arr, spec in zip(extra, extra_specs):
        blk += _nbytes(spec.block_shape, arr.dtype)
    for sh, spec in zip(out_shapes, out_specs):
        blk += _nbytes(spec.block_shape, sh.dtype)
    acc_bytes = _nbytes((tm, tn), F32)
    kern = functools.partial(_mm_kernel, nk=nk, n_extra=len(extra), n_out=len(out_shapes),
                             epilogue=epilogue)
    return pl.pallas_call(
        kern,
        grid=grid,
        in_specs=in_specs,
        out_specs=list(out_specs),
        out_shape=list(out_shapes),
        scratch_shapes=scratch,
        compiler_params=pltpu.CompilerParams(
            dimension_semantics=("parallel", "parallel", "arbitrary"),
            vmem_limit_bytes=_vmem_limit(blk, scratch_bytes=acc_bytes if nk > 1 else 0,
                                         temp_bytes=2 * acc_bytes)),
        name=name,
    )(a, w, *extra)


def _ep_cast(acc, extra, outs):
    outs[0][...] = acc.astype(outs[0].dtype)


def _ep_relu2(acc, extra, outs):
    r = jnp.maximum(acc, 0.0)
    outs[0][...] = (r * r).astype(outs[0].dtype)


def _ep_norm(acc, extra, outs):
    outs[0][...] = (_rms(acc, acc.shape[-1]) * extra[0][...]).astype(outs[0].dtype)


def _ep_residual(acc, extra, outs):
    x_new = extra[0][...] + acc
    outs[0][...] = x_new
    if len(outs) > 1:
        y = _rms(x_new, x_new.shape[-1])
        for g_ref, o_ref in zip(extra[1:], outs[1:]):
            o_ref[...] = (y * g_ref[...]).astype(o_ref.dtype)


def _ep_kv_down(acc, extra, outs):
    g_kv, g_pe, cos_ref, sin_ref = extra
    c = acc[:, :MLA_KV_RANK]
    outs[0][...] = (_rms(c, MLA_KV_RANK) * g_kv[...]).astype(outs[0].dtype)
    pe = acc[:, MLA_KV_RANK:MLA_KV_RANK + LANES]
    outs[1][...] = _norm_rope64(pe, g_pe[...], cos_ref[...], sin_ref[...]).astype(outs[1].dtype)


def _ep_kv_up(acc, extra, outs, *, heads):
    g_nope, kpe_ref = extra
    k_out, v_out = outs
    kpe = kpe_ref[...]
    for h in range(heads):
        base = h * (MLA_NOPE_DIM + MLA_V_DIM)
        kn = acc[:, base:base + MLA_NOPE_DIM]
        k_out[:, h * MLA_HEAD_PAD:h * MLA_HEAD_PAD + MLA_NOPE_DIM] = (
            _rms(kn, MLA_NOPE_DIM) * g_nope[...]).astype(k_out.dtype)
        k_out[:, h * MLA_HEAD_PAD + MLA_NOPE_DIM:(h + 1) * MLA_HEAD_PAD] = kpe
        v_out[:, h * MLA_V_PAD:h * MLA_V_PAD + MLA_V_DIM] = (
            acc[:, base + MLA_NOPE_DIM:base + MLA_NOPE_DIM + MLA_V_DIM]).astype(v_out.dtype)
        v_out[:, h * MLA_V_PAD + MLA_V_DIM:(h + 1) * MLA_V_PAD] = jnp.ones(
            (acc.shape[0], MLA_V_PAD - MLA_V_DIM), v_out.dtype)


def _q_up_kernel(a_ref, w_ref, gn_ref, gp_ref, cos_ref, sin_ref, q_out, acc0_ref, acc1_ref, *,
                 heads, scale):
    t = pl.program_id(0)

    @pl.when(t == 0)
    def _():
        acc1_ref[...] = jnp.zeros_like(acc1_ref)

    def body(new_ref, prev_ref):
        a = a_ref[...]
        g_nope = gn_ref[...] * scale
        g_pe = gp_ref[...] * scale
        cos128, sin128 = cos_ref[...], sin_ref[...]
        for h in range(heads):
            base = h * MLA_HEAD_PAD
            new_ref[:, base:base + MLA_HEAD_PAD] = jnp.dot(
                a, w_ref[:, base:base + MLA_HEAD_PAD], preferred_element_type=F32)
            prev = prev_ref[:, base:base + MLA_HEAD_PAD]
            q_out[:, base:base + MLA_NOPE_DIM] = (
                _rms(prev[:, :MLA_NOPE_DIM], MLA_NOPE_DIM) * g_nope).astype(q_out.dtype)
            q_out[:, base + MLA_NOPE_DIM:base + MLA_HEAD_PAD] = _norm_rope64(
                prev[:, MLA_NOPE_DIM:], g_pe, cos128, sin128).astype(q_out.dtype)

    parity = lax.rem(t, 2)
    pl.when(parity == 0)(functools.partial(body, acc0_ref, acc1_ref))
    pl.when(parity == 1)(functools.partial(body, acc1_ref, acc0_ref))


def _q_up(c_q, w_uq, g_nope, g_pe, cos128, sin128, *, scale, tm=1024, heads_per_tile=4):
    m, kdim = c_q.shape
    n = w_uq.shape[1]
    tn = heads_per_tile * MLA_HEAD_PAD
    nj = n // tn
    n_tiles = (m // tm) * nj
    cur = lambda t: jnp.minimum(t, n_tiles - 1)
    prv = lambda t: jnp.maximum(t - 1, 0)
    blk = (_nbytes((tm, kdim), BF16) + _nbytes((kdim, tn), BF16) + 2 * _nbytes((tm, LANES), F32)
           + _nbytes((tm, tn), BF16))
    return pl.pallas_call(
        functools.partial(_q_up_kernel, heads=heads_per_tile, scale=scale),
        grid=(n_tiles + 1,),
        in_specs=[pl.BlockSpec((tm, kdim), lambda t: (cur(t) // nj, 0)),
                  pl.BlockSpec((kdim, tn), lambda t: (0, cur(t) % nj)),
                  pl.BlockSpec((1, MLA_NOPE_DIM), lambda t: (0, 0)),
                  pl.BlockSpec((1, LANES), lambda t: (0, 0)),
                  pl.BlockSpec((tm, LANES), lambda t: (prv(t) // nj, 0)),
                  pl.BlockSpec((tm, LANES), lambda t: (prv(t) // nj, 0))],
        out_specs=pl.BlockSpec((tm, tn), lambda t: (prv(t) // nj, prv(t) % nj)),
        out_shape=jax.ShapeDtypeStruct((m, n), BF16),
        scratch_shapes=[pltpu.VMEM((tm, tn), F32), pltpu.VMEM((tm, tn), F32)],
        compiler_params=pltpu.CompilerParams(
            dimension_semantics=("arbitrary",),
            vmem_limit_bytes=_vmem_limit(blk, scratch_bytes=_nbytes((2, tm, tn), F32),
                                         temp_bytes=4 * _nbytes((tm, MLA_HEAD_PAD), F32))),
        name="q_up",
    )(c_q, w_uq, g_nope, g_pe, cos128, sin128)


def _mm_wcast_kernel(a_ref, w_ref, o_ref, w_bf_ref, *, epilogue):
    @pl.when(pl.program_id(1) == 0)
    def _():
        w_bf_ref[...] = w_ref[...].astype(w_bf_ref.dtype)

    acc = jnp.dot(a_ref[...], w_bf_ref[...], preferred_element_type=F32)
    epilogue(acc, (), (o_ref,))


def _mm_wcast(a, w_stack, layer, *, tm, tn, out_dtype, epilogue, name):
    m, kdim = a.shape
    n = w_stack.shape[2]
    blk = (_nbytes((tm, kdim), a.dtype) + _nbytes((kdim, tn), w_stack.dtype)
           + _nbytes((tm, tn), out_dtype))
    return pl.pallas_call(
        functools.partial(_mm_wcast_kernel, epilogue=epilogue),
        grid=(n // tn, m // tm),
        in_specs=[pl.BlockSpec((tm, kdim), lambda j, i: (i, 0)),
                  pl.BlockSpec((None, kdim, tn), lambda j, i: (layer, 0, j))],
        out_specs=pl.BlockSpec((tm, tn), lambda j, i: (i, j)),
        out_shape=jax.ShapeDtypeStruct((m, n), out_dtype),
        scratch_shapes=[pltpu.VMEM((kdim, tn), BF16)],
        compiler_params=pltpu.CompilerParams(
            dimension_semantics=("arbitrary", "arbitrary"),
            vmem_limit_bytes=_vmem_limit(blk, scratch_bytes=_nbytes((kdim, tn), BF16),
                                         temp_bytes=2 * _nbytes((tm, tn), F32))),
        name=name,
    )(a, w_stack)


def _row_spec(tm, width):
    return pl.BlockSpec((tm, width), lambda i, j, k: (i, 0))


def _gain_spec(width):
    return pl.BlockSpec((1, width), lambda i, j, k: (0, 0))


def _mm_residual(a, w_stack, layer, res, gains, *, tm, name):
    m, (kdim, n) = a.shape[0], w_stack.shape[1:]
    outs = [jax.ShapeDtypeStruct((m, n), F32)] + [jax.ShapeDtypeStruct((m, n), BF16)] * len(gains)
    return _matmul(a, w_stack, layer=layer, tm=tm, tn=n, tk=kdim, epilogue=_ep_residual,
                   extra=[res] + list(gains),
                   extra_specs=[_row_spec(tm, n)] + [_gain_spec(n)] * len(gains),
                   out_shapes=outs, out_specs=[_row_spec(tm, n)] * len(outs), name=name)


def _retention_kernel(lg_ref, q_ref, k_ref, v_ref, cos_ref, sin_ref, o_ref, state_ref, *, rows):
    c = RET_CHUNK
    half = RET_QK_DIM // 2

    @pl.when(pl.program_id(2) == 0)
    def _():
        state_ref[...] = jnp.zeros_like(state_ref)

    lg = lg_ref[0]
    n_row = lax.broadcasted_iota(jnp.int32, (c, c), 0).astype(F32)
    n_col = lax.broadcasted_iota(jnp.int32, (c, c), 1).astype(F32)
    diff = n_row - n_col
    decay = jnp.where(diff >= 0, jnp.exp(jnp.maximum(diff, 0.0) * lg), 0.0)
    xi = jnp.exp((n_row + 1.0) * lg)
    zeta = jnp.exp((c - 1.0 - n_row) * lg)
    chunk_decay = jnp.exp(c * lg)[:, :1]
    k_scale = RET_QK_DIM ** -0.5

    for ci in range(rows // c):
        sl = slice(ci * c, (ci + 1) * c)
        cos, sin = cos_ref[sl, :], sin_ref[sl, :]
        q = q_ref[sl, :].astype(F32)
        k = k_ref[sl, :].astype(F32)
        q1, q2 = q[:, :half], q[:, half:]
        k1, k2 = k[:, :half], k[:, half:]
        qr1, qr2 = q1 * cos - q2 * sin, q2 * cos + q1 * sin
        kr1, kr2 = (k1 * cos - k2 * sin) * k_scale, (k2 * cos + k1 * sin) * k_scale
        qr = jnp.concatenate([qr1, qr2], axis=1).astype(BF16)
        kr = jnp.concatenate([kr1, kr2], axis=1).astype(BF16)
        qs = jnp.concatenate([qr1 * xi, qr2 * xi], axis=1).astype(BF16)
        ks = jnp.concatenate([kr1 * zeta, kr2 * zeta], axis=1).astype(BF16)
        v = v_ref[sl, :]

        scores = lax.dot_general(qr, kr, (((1,), (1,)), ((), ())), preferred_element_type=F32)
        scores = scores * decay
        state = state_ref[...]
        o = jnp.dot(scores.astype(BF16), v, preferred_element_type=F32)
        o = o + jnp.dot(qs, state.astype(BF16), preferred_element_type=F32)
        state_ref[...] = chunk_decay * state + lax.dot_general(
            ks, v, (((0,), (0,)), ((), ())), preferred_element_type=F32)

        o_ref[sl, :] = o.astype(o_ref.dtype)


def _retention(proj, cos, sin, log_gamma, *, batch, seq, rows=1024):
    m = proj.shape[0]
    nt = seq // rows
    hq = RET_HEADS * RET_QK_DIM // RET_QK_DIM
    hv = 2 * RET_HEADS * RET_QK_DIM // RET_V_DIM
    row = lambda b, h, t: b * nt + t
    in_specs = [
        pl.BlockSpec((1, 1, LANES), lambda b, h, t: (h, 0, 0)),
        pl.BlockSpec((rows, RET_QK_DIM), lambda b, h, t: (row(b, h, t), h)),
        pl.BlockSpec((rows, RET_QK_DIM), lambda b, h, t: (row(b, h, t), hq + h)),
        pl.BlockSpec((rows, RET_V_DIM), lambda b, h, t: (row(b, h, t), hv + h)),
        pl.BlockSpec((rows, LANES), lambda b, h, t: (row(b, h, t), 0)),
        pl.BlockSpec((rows, LANES), lambda b, h, t: (row(b, h, t), 0)),
    ]
    blk = (2 * _nbytes((rows, RET_QK_DIM), BF16) + 2 * _nbytes((rows, RET_V_DIM), BF16)
           + 2 * _nbytes((rows, LANES), F32))
    return pl.pallas_call(
        functools.partial(_retention_kernel, rows=rows),
        grid=(batch, RET_HEADS, nt),
        in_specs=in_specs,
        out_specs=pl.BlockSpec((rows, RET_V_DIM), lambda b, h, t: (row(b, h, t), h)),
        out_shape=jax.ShapeDtypeStruct((m, RET_HEADS * RET_V_DIM), BF16),
        scratch_shapes=[pltpu.VMEM((RET_QK_DIM, RET_V_DIM), F32)],
        compiler_params=pltpu.CompilerParams(
            dimension_semantics=("parallel", "parallel", "arbitrary"),
            vmem_limit_bytes=_vmem_limit(blk, scratch_bytes=_nbytes((RET_QK_DIM, RET_V_DIM), F32),
                                         temp_bytes=8 * 1024 * 1024)),
        name="retention",
    )(log_gamma, proj, proj, proj, cos, sin)


def _ret_out_kernel(o_ref, g_ref, w_ref, res_ref, gain_ref, x_out, h_out):
    acc = None
    for h in range(RET_HEADS):
        cols = slice(h * RET_V_DIM, (h + 1) * RET_V_DIM)
        gate = g_ref[:, cols].astype(F32)
        a = gate * jax.nn.sigmoid(gate) * _rms(o_ref[:, cols].astype(F32), RET_V_DIM)
        d = jnp.dot(a.astype(BF16), w_ref[cols, :], preferred_element_type=F32)
        acc = d if acc is None else acc + d
    _ep_residual(acc, (res_ref, gain_ref), (x_out, h_out))


def _ret_out(o_raw, proj, w_stack, layer, res, gain, *, tm=256):
    m, kdim = o_raw.shape
    n = w_stack.shape[2]
    gate_block = proj.shape[1] // kdim - 1
    blk = (2 * _nbytes((tm, kdim), BF16) + _nbytes((kdim, n), BF16) // 2
           + 2 * _nbytes((tm, n), F32) + _nbytes((tm, n), BF16))
    return pl.pallas_call(
        _ret_out_kernel,
        grid=(m // tm,),
        in_specs=[pl.BlockSpec((tm, kdim), lambda i: (i, 0)),
                  pl.BlockSpec((tm, kdim), lambda i: (i, gate_block)),
                  pl.BlockSpec((None, kdim, n), lambda i: (layer, 0, 0),
                               pipeline_mode=pl.Buffered(1)),
                  pl.BlockSpec((tm, n), lambda i: (i, 0)),
                  pl.BlockSpec((1, n), lambda i: (0, 0))],
        out_specs=[pl.BlockSpec((tm, n), lambda i: (i, 0)), pl.BlockSpec((tm, n), lambda i: (i, 0))],
        out_shape=[jax.ShapeDtypeStruct((m, n), F32), jax.ShapeDtypeStruct((m, n), BF16)],
        compiler_params=pltpu.CompilerParams(
            dimension_semantics=("parallel",),
            vmem_limit_bytes=_vmem_limit(blk, temp_bytes=3 * _nbytes((tm, n), F32))),
        name="ret_out",
    )(o_raw, proj, w_stack, res, gain)


def _flash_kernel(q_ref, k_ref, v_ref, o_ref, m_ref, acc_ref, *, tq, sub, kw, hp, unroll, nq):
    qi = pl.program_id(2)
    nsub = tq // sub
    m_ref[...] = jnp.full_like(m_ref, -jnp.inf)
    acc_ref[...] = jnp.zeros_like(acc_ref)

    def step(hh, si, start, width, diag_offset):
        rows = slice(si * sub, (si + 1) * sub)
        qk_cols = slice(hh * MLA_HEAD_PAD, (hh + 1) * MLA_HEAD_PAD)
        v_cols = slice(hh * MLA_V_PAD, (hh + 1) * MLA_V_PAD)
        q = q_ref[rows, qk_cols]
        k = k_ref[pl.ds(start, width), qk_cols]
        v = v_ref[pl.ds(start, width), v_cols]
        s = lax.dot_general(q, k, (((1,), (1,)), ((), ())), preferred_element_type=F32)
        if diag_offset is not None:
            r = lax.broadcasted_iota(jnp.int32, (sub, width), 0)
            c = lax.broadcasted_iota(jnp.int32, (sub, width), 1)
            s = jnp.where(c <= r + diag_offset, s, NEG_BIG)
        m_prev = m_ref[hh, rows, :]
        m_next = jnp.maximum(m_prev, jnp.max(s, axis=-1, keepdims=True))
        alpha = jnp.exp2(m_prev - m_next)
        p = jnp.exp2(s - jnp.concatenate([m_next] * (width // LANES), axis=1))
        acc_ref[hh, rows, :] = (
            jnp.concatenate([alpha] * (MLA_V_PAD // LANES), axis=1) * acc_ref[hh, rows, :]
            + jnp.dot(p.astype(v.dtype), v, preferred_element_type=F32))
        m_ref[hh, rows, :] = m_next

    def full_chunks(first, count):
        for u in range(count):
            start = pl.multiple_of((first + u) * kw, kw)
            for hh in range(hp):
                for si in range(nsub):
                    step(hh, si, start, kw, None)

    def chunk_group(i, carry):
        full_chunks(i * unroll, unroll)
        return carry

    n_full = qi * (tq // kw)
    lax.fori_loop(0, n_full // unroll, chunk_group, 0)

    def tail(rem):
        full_chunks(n_full - rem, rem)
        tile0 = pl.multiple_of(qi * tq, tq)
        for hh in range(hp):
            for si in range(nsub):
                for kj in range(si + 1):
                    step(hh, si, tile0 + kj * sub, sub, 0 if kj == si else None)
        for hh in range(hp):
            acc = acc_ref[hh]
            o_ref[:, hh * MLA_V_DIM:(hh + 1) * MLA_V_DIM] = (
                acc[:, :MLA_V_DIM] / acc[:, MLA_V_DIM:]).astype(o_ref.dtype)

    for rem in sorted({(i * (tq // kw)) % unroll for i in range(nq)}):
        pl.when(n_full % unroll == rem)(functools.partial(tail, rem))


def _flash(q, k, v, *, batch, seq, heads, tq=2048, sub=512, kw=1024, hp=2, unroll=2):
    m = q.shape[0]
    nq = seq // tq
    blk = (_nbytes((tq, hp * MLA_HEAD_PAD), BF16) + _nbytes((seq, hp * MLA_HEAD_PAD), BF16)
           + _nbytes((seq, hp * MLA_V_PAD), BF16) + _nbytes((tq, hp * MLA_V_DIM), BF16))
    scratch = [pltpu.VMEM((hp, tq, LANES), F32), pltpu.VMEM((hp, tq, MLA_V_PAD), F32)]
    return pl.pallas_call(
        functools.partial(_flash_kernel, tq=tq, sub=sub, kw=kw, hp=hp, unroll=unroll, nq=nq),
        grid=(batch, heads // hp, nq),
        in_specs=[
            pl.BlockSpec((tq, hp * MLA_HEAD_PAD), lambda b, h, i: (b * nq + i, h)),
            pl.BlockSpec((seq, hp * MLA_HEAD_PAD), lambda b, h, i: (b, h)),
            pl.BlockSpec((seq, hp * MLA_V_PAD), lambda b, h, i: (b, h)),
        ],
        out_specs=pl.BlockSpec((tq, hp * MLA_V_DIM), lambda b, h, i: (b * nq + i, h)),
        out_shape=jax.ShapeDtypeStruct((m, heads * MLA_V_DIM), BF16),
        scratch_shapes=scratch,
        compiler_params=pltpu.CompilerParams(
            dimension_semantics=("parallel", "parallel", "arbitrary"),
            vmem_limit_bytes=_vmem_limit(
                blk, scratch_bytes=hp * _nbytes((tq, LANES + MLA_V_PAD), F32),
                temp_bytes=hp * (tq // sub) * _nbytes((sub, kw), F32))),
        name="flash_attention",
    )(q, k, v)


def _rope_tables(positions, dim):
    inv_freq = 1.0 / (ROPE_BASE ** (jnp.arange(0, dim, 2, dtype=F32) / dim))
    ang = positions.astype(F32)[..., None] * inv_freq
    return jnp.cos(ang), jnp.sin(ang)


def kernel(x, positions, norm_mix, norm_mlp, ret_w_in, ret_w_out, kv_norm_in, mla_w_kv_down,
           mla_kv_norm, mla_w_kv_up, mla_k_nope_norm, mla_k_pe_norm, mla_w_dq, mla_q_norm,
           mla_w_uq, mla_q_nope_norm, mla_q_pe_norm, mla_w_o, mlp_w1, mlp_w2):
    batch, seq, d_model = x.shape
    m = batch * seq
    depth = norm_mix.shape[0]
    n_ret = ret_w_in.shape[0]
    mla_heads = mla_w_o.shape[1] // MLA_V_DIM
    row = lambda g: g.reshape(1, -1).astype(F32)

    rc, rs = _rope_tables(positions, RET_QK_DIM)
    ret_cos, ret_sin = rc.reshape(m, -1), rs.reshape(m, -1)
    mc, ms_ = _rope_tables(positions, MLA_ROPE_DIM)
    mc, ms_ = mc.reshape(m, -1), ms_.reshape(m, -1)
    zpad = jnp.zeros((m, LANES - MLA_ROPE_DIM), F32)
    mla_cos = jnp.concatenate([mc, mc, zpad], axis=1)
    mla_sin = jnp.concatenate([-ms_, ms_, zpad], axis=1)
    log_gamma = jnp.log1p(-jnp.exp2(-5.0 - jnp.arange(RET_HEADS, dtype=F32)))
    log_gamma = jnp.broadcast_to(log_gamma[:, None, None], (RET_HEADS, 1, LANES))

    def swap_halves(t):
        half = MLA_ROPE_DIM // 2
        return jnp.concatenate([t[..., half:], t[..., :half]], axis=-1)

    def pe_gain(g):
        g = g.astype(F32)
        return jnp.concatenate([g, swap_halves(g)])[None, :]

    w2_bf, ret_out_bf = mlp_w2.astype(BF16), ret_w_out.astype(BF16)
    w_o_bf, w_dq_bf = mla_w_o.astype(BF16), mla_w_dq.astype(BF16)

    xf = x.reshape(m, d_model)
    h = _rmsnorm(xf, row(norm_mix[0]))
    h_kv = None

    def mlp(xf, h, layer, next_gains):
        u = _mm_wcast(h, mlp_w1, layer, tm=2048, tn=1024, out_dtype=BF16,
                      epilogue=_ep_relu2, name="mlp_up")
        return _mm_residual(u, w2_bf, layer, xf, next_gains, tm=256, name="mlp_down")

    for layer in range(depth):
        if layer < n_ret:
            proj = _mm_wcast(h, ret_w_in, layer, tm=2048, tn=1024, out_dtype=BF16,
                             epilogue=_ep_cast, name="ret_proj")
            r = _retention(proj, ret_cos, ret_sin, log_gamma, batch=batch, seq=seq)
            xf, h = _ret_out(r, proj, ret_out_bf, layer, xf, row(norm_mlp[layer]))
        else:
            j = layer - n_ret
            if j == 0:
                w_down = jnp.concatenate(
                    [mla_w_kv_down, swap_halves(mla_w_kv_down[:, MLA_KV_RANK:])], axis=1).astype(BF16)
                n_down = w_down.shape[1]
                c_kv, k_pe = _matmul(
                    h_kv, w_down, tm=1024, tn=n_down, tk=d_model, epilogue=_ep_kv_down,
                    extra=[row(mla_kv_norm), pe_gain(mla_k_pe_norm), mla_cos, mla_sin],
                    extra_specs=[_gain_spec(MLA_KV_RANK), _gain_spec(LANES),
                                 _row_spec(1024, LANES), _row_spec(1024, LANES)],
                    out_shapes=[jax.ShapeDtypeStruct((m, MLA_KV_RANK), BF16),
                                jax.ShapeDtypeStruct((m, LANES), BF16)],
                    out_specs=[_row_spec(1024, MLA_KV_RANK), _row_spec(1024, LANES)],
                    name="kv_down")
                hpt = 4
                k_all, v_all = _matmul(
                    c_kv, mla_w_kv_up.astype(BF16), tm=1024,
                    tn=hpt * (MLA_NOPE_DIM + MLA_V_DIM), tk=MLA_KV_RANK,
                    epilogue=functools.partial(_ep_kv_up, heads=hpt),
                    extra=[row(mla_k_nope_norm), k_pe],
                    extra_specs=[_gain_spec(MLA_NOPE_DIM), _row_spec(1024, LANES)],
                    out_shapes=[jax.ShapeDtypeStruct((m, mla_heads * MLA_HEAD_PAD), BF16),
                                jax.ShapeDtypeStruct((m, mla_heads * MLA_V_PAD), BF16)],
                    out_specs=[pl.BlockSpec((1024, hpt * MLA_HEAD_PAD), lambda i, j, k: (i, j)),
                               pl.BlockSpec((1024, hpt * MLA_V_PAD), lambda i, j, k: (i, j))],
                    name="kv_up")
            q_rank = mla_w_dq.shape[2]
            c_q = _matmul(h, w_dq_bf, layer=j, tm=1024, tn=q_rank, tk=d_model,
                          epilogue=_ep_norm, extra=[row(mla_q_norm[j])],
                          extra_specs=[_gain_spec(q_rank)],
                          out_shapes=[jax.ShapeDtypeStruct((m, q_rank), BF16)],
                          out_specs=[_row_spec(1024, q_rank)], name="q_down")[0]
            qk_dim = MLA_NOPE_DIM + MLA_ROPE_DIM
            w_uq = mla_w_uq[j].reshape(q_rank, mla_heads, qk_dim)
            w_uq = jnp.concatenate([w_uq, swap_halves(w_uq[..., MLA_NOPE_DIM:])], axis=-1)
            w_uq = w_uq.reshape(q_rank, mla_heads * MLA_HEAD_PAD).astype(BF16)
            q_all = _q_up(c_q, w_uq, row(mla_q_nope_norm[j]), pe_gain(mla_q_pe_norm[j]),
                          mla_cos, mla_sin, scale=qk_dim ** -0.5 * LOG2_E)
            o = _flash(q_all, k_all, v_all, batch=batch, seq=seq, heads=mla_heads)
            xf, h = _mm_residual(o, w_o_bf, j, xf, [row(norm_mlp[layer])], tm=512,
                                 name="attn_out")

        if layer + 1 == depth:
            next_gains = []
        elif layer + 1 == n_ret:
            next_gains = [row(norm_mix[layer + 1]), row(kv_norm_in)]
        else:
            next_gains = [row(norm_mix[layer + 1])]
        outs = mlp(xf, h, layer, next_gains)
        xf = outs[0]
        if layer + 1 == n_ret:
            h, h_kv = outs[1], outs[2]
        elif layer + 1 < depth:
            h = outs[1]

    return xf.reshape(batch, seq, d_model)
```

```python
import functools

import jax
import jax.numpy as jnp
from jax import lax
from jax.experimental import pallas as pl
from jax.experimental.pallas import tpu as pltpu

F32 = jnp.float32
BF16 = jnp.bfloat16

RMS_EPS = 1e-6
ROPE_BASE = 10000.0

RET_HEADS = 8
RET_QK_DIM = 256
RET_V_DIM = 512
RET_CHUNK = 256

MLA_NOPE_DIM = 128
MLA_ROPE_DIM = 64
MLA_V_DIM = 128
MLA_KV_RANK = 512
MLA_HEAD_PAD = 256
MLA_V_PAD = 256

LANES = 128
V7X_VMEM_BYTES = 64 * 1024 * 1024
NEG_BIG = -1e30
LOG2_E = 1.4426950408889634


def _vmem_limit(block_bytes, scratch_bytes=0, temp_bytes=0):
    need = 2 * block_bytes + scratch_bytes + temp_bytes
    return int(min(max(need + need // 4, 16 * 1024 * 1024), V7X_VMEM_BYTES - 6 * 1024 * 1024))


def _nbytes(shape, dtype):
    n = 1
    for s in shape:
        n *= s
    return n * jnp.dtype(dtype).itemsize


def _rms(x, width):
    ms = jnp.sum(x * x, axis=-1, keepdims=True) * (1.0 / width)
    return x * lax.rsqrt(ms + RMS_EPS)


def _norm_rope64(slab, gain128, cos128, sin128):
    y = _rms(slab, LANES) * gain128
    return y * cos128 + pltpu.roll(y, MLA_ROPE_DIM, 1) * sin128


def _rmsnorm_kernel(x_ref, g_ref, o_ref):
    x = x_ref[...].astype(F32)
    o_ref[...] = (_rms(x, x.shape[-1]) * g_ref[...]).astype(o_ref.dtype)


def _rmsnorm(x, gain, *, tm=512):
    m, d = x.shape
    return pl.pallas_call(
        _rmsnorm_kernel,
        grid=(m // tm,),
        in_specs=[pl.BlockSpec((tm, d), lambda i: (i, 0)), pl.BlockSpec((1, d), lambda i: (0, 0))],
        out_specs=pl.BlockSpec((tm, d), lambda i: (i, 0)),
        out_shape=jax.ShapeDtypeStruct((m, d), BF16),
        compiler_params=pltpu.CompilerParams(
            dimension_semantics=("parallel",),
            vmem_limit_bytes=_vmem_limit(_nbytes((tm, d), x.dtype) + _nbytes((tm, d), BF16),
                                         temp_bytes=2 * _nbytes((tm, d), F32))),
        name="rmsnorm",
    )(x, gain)


def _mm_kernel(*refs, nk, n_extra, n_out, epilogue):
    a_ref, w_ref = refs[0], refs[1]
    extra = refs[2:2 + n_extra]
    outs = refs[2 + n_extra:2 + n_extra + n_out]
    if nk == 1:
        acc = jnp.dot(a_ref[...], w_ref[...], preferred_element_type=F32)
        epilogue(acc, extra, outs)
        return
    acc_ref = refs[2 + n_extra + n_out]
    k = pl.program_id(2)

    @pl.when(k == 0)
    def _():
        acc_ref[...] = jnp.zeros_like(acc_ref)

    acc_ref[...] += jnp.dot(a_ref[...], w_ref[...], preferred_element_type=F32)

    @pl.when(k == nk - 1)
    def _():
        epilogue(acc_ref[...], extra, outs)


def _matmul(a, w, *, tm, tn, tk, epilogue, extra=(), extra_specs=(), out_shapes, out_specs, name,
            layer=None):
    m, kdim = a.shape
    n = w.shape[-1]
    nk = kdim // tk
    grid = (m // tm, n // tn, nk)
    w_whole = nk == 1 and n == tn
    w_mode = pl.Buffered(1) if w_whole else None
    if layer is None:
        w_spec = pl.BlockSpec((tk, tn), lambda i, j, k: (k, j), pipeline_mode=w_mode)
    else:
        w_spec = pl.BlockSpec((None, tk, tn), lambda i, j, k: (layer, k, j), pipeline_mode=w_mode)
    in_specs = [pl.BlockSpec((tm, tk), lambda i, j, k: (i, k)), w_spec] + list(extra_specs)
    scratch = [pltpu.VMEM((tm, tn), F32)] if nk > 1 else []
    blk = _nbytes((tm, tk), a.dtype) + _nbytes((tk, tn), w.dtype) // (2 if w_whole else 1)
    for arr, spec in zip(extra, extra_specs):
        blk += _nbytes(spec.block_shape, arr.dtype)
    for sh, spec in zip(out_shapes, out_specs):
        blk += _nbytes(spec.block_shape, sh.dtype)
    acc_bytes = _nbytes((tm, tn), F32)
    kern = functools.partial(_mm_kernel, nk=nk, n_extra=len(extra), n_out=len(out_shapes),
                             epilogue=epilogue)
    return pl.pallas_call(
        kern,
        grid=grid,
        in_specs=in_specs,
        out_specs=list(out_specs),
        out_shape=list(out_shapes),
        scratch_shapes=scratch,
        compiler_params=pltpu.CompilerParams(
            dimension_semantics=("parallel", "parallel", "arbitrary"),
            vmem_limit_bytes=_vmem_limit(blk, scratch_bytes=acc_bytes if nk > 1 else 0,
                                         temp_bytes=2 * acc_bytes)),
        name=name,
    )(a, w, *extra)


def _ep_cast(acc, extra, outs):
    outs[0][...] = acc.astype(outs[0].dtype)


def _ep_relu2(acc, extra, outs):
    r = jnp.maximum(acc, 0.0)
    outs[0][...] = (r * r).astype(outs[0].dtype)


def _ep_norm(acc, extra, outs):
    outs[0][...] = (_rms(acc, acc.shape[-1]) * extra[0][...]).astype(outs[0].dtype)


def _ep_residual(acc, extra, outs):
    x_new = extra[0][...] + acc
    outs[0][...] = x_new
    if len(outs) > 1:
        y = _rms(x_new, x_new.shape[-1])
        for g_ref, o_ref in zip(extra[1:], outs[1:]):
            o_ref[...] = (y * g_ref[...]).astype(o_ref.dtype)


def _ep_kv_down(acc, extra, outs):
    g_kv, g_pe, cos_ref, sin_ref = extra
    c = acc[:, :MLA_KV_RANK]
    outs[0][...] = (_rms(c, MLA_KV_RANK) * g_kv[...]).astype(outs[0].dtype)
    pe = acc[:, MLA_KV_RANK:MLA_KV_RANK + LANES]
    outs[1][...] = _norm_rope64(pe, g_pe[...], cos_ref[...], sin_ref[...]).astype(outs[1].dtype)


def _ep_kv_up(acc, extra, outs, *, heads):
    g_nope, kpe_ref = extra
    k_out, v_out = outs
    kpe = kpe_ref[...]
    for h in range(heads):
        base = h * (MLA_NOPE_DIM + MLA_V_DIM)
        kn = acc[:, base:base + MLA_NOPE_DIM]
        k_out[:, h * MLA_HEAD_PAD:h * MLA_HEAD_PAD + MLA_NOPE_DIM] = (
            _rms(kn, MLA_NOPE_DIM) * g_nope[...]).astype(k_out.dtype)
        k_out[:, h * MLA_HEAD_PAD + MLA_NOPE_DIM:(h + 1) * MLA_HEAD_PAD] = kpe
        v_out[:, h * MLA_V_PAD:h * MLA_V_PAD + MLA_V_DIM] = (
            acc[:, base + MLA_NOPE_DIM:base + MLA_NOPE_DIM + MLA_V_DIM]).astype(v_out.dtype)
        v_out[:, h * MLA_V_PAD + MLA_V_DIM:(h + 1) * MLA_V_PAD] = jnp.ones(
            (acc.shape[0], MLA_V_PAD - MLA_V_DIM), v_out.dtype)


def _q_up_kernel(a_ref, w_ref, gn_ref, gp_ref, cos_ref, sin_ref, q_out, acc0_ref, acc1_ref, *,
                 heads, scale):
    t = pl.program_id(0)

    @pl.when(t == 0)
    def _():
        acc1_ref[...] = jnp.zeros_like(acc1_ref)

    def body(new_ref, prev_ref):
        a = a_ref[...]
        g_nope = gn_ref[...] * scale
        g_pe = gp_ref[...] * scale
        cos128, sin128 = cos_ref[...], sin_ref[...]
        for h in range(heads):
            base = h * MLA_HEAD_PAD
            new_ref[:, base:base + MLA_HEAD_PAD] = jnp.dot(
                a, w_ref[:, base:base + MLA_HEAD_PAD], preferred_element_type=F32)
            prev = prev_ref[:, base:base + MLA_HEAD_PAD]
            q_out[:, base:base + MLA_NOPE_DIM] = (
                _rms(prev[:, :MLA_NOPE_DIM], MLA_NOPE_DIM) * g_nope).astype(q_out.dtype)
            q_out[:, base + MLA_NOPE_DIM:base + MLA_HEAD_PAD] = _norm_rope64(
                prev[:, MLA_NOPE_DIM:], g_pe, cos128, sin128).astype(q_out.dtype)

    parity = lax.rem(t, 2)
    pl.when(parity == 0)(functools.partial(body, acc0_ref, acc1_ref))
    pl.when(parity == 1)(functools.partial(body, acc1_ref, acc0_ref))


def _q_up(c_q, w_uq, g_nope, g_pe, cos128, sin128, *, scale, tm=1024, heads_per_tile=4):
    m, kdim = c_q.shape
    n = w_uq.shape[1]
    tn = heads_per_tile * MLA_HEAD_PAD
    nj = n // tn
    n_tiles = (m // tm) * nj
    cur = lambda t: jnp.minimum(t, n_tiles - 1)
    prv = lambda t: jnp.maximum(t - 1, 0)
    blk = (_nbytes((tm, kdim), BF16) + _nbytes((kdim, tn), BF16) + 2 * _nbytes((tm, LANES), F32)
           + _nbytes((tm, tn), BF16))
    return pl.pallas_call(
        functools.partial(_q_up_kernel, heads=heads_per_tile, scale=scale),
        grid=(n_tiles + 1,),
        in_specs=[pl.BlockSpec((tm, kdim), lambda t: (cur(t) // nj, 0)),
                  pl.BlockSpec((kdim, tn), lambda t: (0, cur(t) % nj)),
                  pl.BlockSpec((1, MLA_NOPE_DIM), lambda t: (0, 0)),
                  pl.BlockSpec((1, LANES), lambda t: (0, 0)),
                  pl.BlockSpec((tm, LANES), lambda t: (prv(t) // nj, 0)),
                  pl.BlockSpec((tm, LANES), lambda t: (prv(t) // nj, 0))],
        out_specs=pl.BlockSpec((tm, tn), lambda t: (prv(t) // nj, prv(t) % nj)),
        out_shape=jax.ShapeDtypeStruct((m, n), BF16),
        scratch_shapes=[pltpu.VMEM((tm, tn), F32), pltpu.VMEM((tm, tn), F32)],
        compiler_params=pltpu.CompilerParams(
            dimension_semantics=("arbitrary",),
            vmem_limit_bytes=_vmem_limit(blk, scratch_bytes=_nbytes((2, tm, tn), F32),
                                         temp_bytes=4 * _nbytes((tm, MLA_HEAD_PAD), F32))),
        name="q_up",
    )(c_q, w_uq, g_nope, g_pe, cos128, sin128)


def _mm_wcast_kernel(a_ref, w_ref, o_ref, w_bf_ref, *, epilogue):
    @pl.when(pl.program_id(1) == 0)
    def _():
        w_bf_ref[...] = w_ref[...].astype(w_bf_ref.dtype)

    acc = jnp.dot(a_ref[...], w_bf_ref[...], preferred_element_type=F32)
    epilogue(acc, (), (o_ref,))


def _mm_wcast(a, w_stack, layer, *, tm, tn, out_dtype, epilogue, name):
    m, kdim = a.shape
    n = w_stack.shape[2]
    blk = (_nbytes((tm, kdim), a.dtype) + _nbytes((kdim, tn), w_stack.dtype)
           + _nbytes((tm, tn), out_dtype))
    return pl.pallas_call(
        functools.partial(_mm_wcast_kernel, epilogue=epilogue),
        grid=(n // tn, m // tm),
        in_specs=[pl.BlockSpec((tm, kdim), lambda j, i: (i, 0)),
                  pl.BlockSpec((None, kdim, tn), lambda j, i: (layer, 0, j))],
        out_specs=pl.BlockSpec((tm, tn), lambda j, i: (i, j)),
        out_shape=jax.ShapeDtypeStruct((m, n), out_dtype),
        scratch_shapes=[pltpu.VMEM((kdim, tn), BF16)],
        compiler_params=pltpu.CompilerParams(
            dimension_semantics=("arbitrary", "arbitrary"),
            vmem_limit_bytes=_vmem_limit(blk, scratch_bytes=_nbytes((kdim, tn), BF16),
                                         temp_bytes=2 * _nbytes((tm, tn), F32))),
        name=name,
    )(a, w_stack)


def _row_spec(tm, width):
    return pl.BlockSpec((tm, width), lambda i, j, k: (i, 0))


def _gain_spec(width):
    return pl.BlockSpec((1, width), lambda i, j, k: (0, 0))


def _mm_residual(a, w_stack, layer, res, gains, *, tm, name):
    m, (kdim, n) = a.shape[0], w_stack.shape[1:]
    outs = [jax.ShapeDtypeStruct((m, n), F32)] + [jax.ShapeDtypeStruct((m, n), BF16)] * len(gains)
    return _matmul(a, w_stack, layer=layer, tm=tm, tn=n, tk=kdim, epilogue=_ep_residual,
                   extra=[res] + list(gains),
                   extra_specs=[_row_spec(tm, n)] + [_gain_spec(n)] * len(gains),
                   out_shapes=outs, out_specs=[_row_spec(tm, n)] * len(outs), name=name)


def _retention_kernel(lg_ref, q_ref, k_ref, v_ref, cos_ref, sin_ref, o_ref, state_ref, *, rows):
    c = RET_CHUNK
    half = RET_QK_DIM // 2

    @pl.when(pl.program_id(2) == 0)
    def _():
        state_ref[...] = jnp.zeros_like(state_ref)

    lg = lg_ref[0]
    n_row = lax.broadcasted_iota(jnp.int32, (c, c), 0).astype(F32)
    n_col = lax.broadcasted_iota(jnp.int32, (c, c), 1).astype(F32)
    diff = n_row - n_col
    decay = jnp.where(diff >= 0, jnp.exp(jnp.maximum(diff, 0.0) * lg[:, :1]), 0.0)
    n_row = n_row[:, :LANES]
    xi = jnp.exp((n_row + 1.0) * lg)
    zeta = jnp.exp((c - 1.0 - n_row) * lg)
    chunk_decay = jnp.exp(c * lg)[:, :1]
    k_scale = RET_QK_DIM ** -0.5

    for ci in range(rows // c):
        sl = slice(ci * c, (ci + 1) * c)
        cos, sin = cos_ref[sl, :], sin_ref[sl, :]
        q = q_ref[sl, :].astype(F32)
        k = k_ref[sl, :].astype(F32)
        q1, q2 = q[:, :half], q[:, half:]
        k1, k2 = k[:, :half], k[:, half:]
        qr1, qr2 = q1 * cos - q2 * sin, q2 * cos + q1 * sin
        kr1, kr2 = (k1 * cos - k2 * sin) * k_scale, (k2 * cos + k1 * sin) * k_scale
        qr = jnp.concatenate([qr1, qr2], axis=1).astype(BF16)
        kr = jnp.concatenate([kr1, kr2], axis=1).astype(BF16)
        qs = jnp.concatenate([qr1 * xi, qr2 * xi], axis=1).astype(BF16)
        ks = jnp.concatenate([kr1 * zeta, kr2 * zeta], axis=1).astype(BF16)
        v = v_ref[sl, :]

        scores = lax.dot_general(qr, kr, (((1,), (1,)), ((), ())), preferred_element_type=F32)
        scores = scores * decay
        state = state_ref[...]
        o = jnp.dot(scores.astype(BF16), v, preferred_element_type=F32)
        o = o + jnp.dot(qs, state.astype(BF16), preferred_element_type=F32)
        state_ref[...] = chunk_decay * state + lax.dot_general(
            ks, v, (((0,), (0,)), ((), ())), preferred_element_type=F32)

        o_ref[sl, :] = o.astype(o_ref.dtype)


def _retention(proj, cos, sin, log_gamma, *, batch, seq, rows=2048):
    m = proj.shape[0]
    nt = seq // rows
    hq = RET_HEADS * RET_QK_DIM // RET_QK_DIM
    hv = 2 * RET_HEADS * RET_QK_DIM // RET_V_DIM
    row = lambda b, h, t: b * nt + t
    in_specs = [
        pl.BlockSpec((1, 1, LANES), lambda b, h, t: (h, 0, 0)),
        pl.BlockSpec((rows, RET_QK_DIM), lambda b, h, t: (row(b, h, t), h)),
        pl.BlockSpec((rows, RET_QK_DIM), lambda b, h, t: (row(b, h, t), hq + h)),
        pl.BlockSpec((rows, RET_V_DIM), lambda b, h, t: (row(b, h, t), hv + h)),
        pl.BlockSpec((rows, LANES), lambda b, h, t: (row(b, h, t), 0)),
        pl.BlockSpec((rows, LANES), lambda b, h, t: (row(b, h, t), 0)),
    ]
    blk = (2 * _nbytes((rows, RET_QK_DIM), BF16) + 2 * _nbytes((rows, RET_V_DIM), BF16)
           + 2 * _nbytes((rows, LANES), F32))
    return pl.pallas_call(
        functools.partial(_retention_kernel, rows=rows),
        grid=(batch, RET_HEADS, nt),
        in_specs=in_specs,
        out_specs=pl.BlockSpec((rows, RET_V_DIM), lambda b, h, t: (row(b, h, t), h)),
        out_shape=jax.ShapeDtypeStruct((m, RET_HEADS * RET_V_DIM), BF16),
        scratch_shapes=[pltpu.VMEM((RET_QK_DIM, RET_V_DIM), F32)],
        compiler_params=pltpu.CompilerParams(
            dimension_semantics=("parallel", "parallel", "arbitrary"),
            vmem_limit_bytes=_vmem_limit(blk, scratch_bytes=_nbytes((RET_QK_DIM, RET_V_DIM), F32),
                                         temp_bytes=8 * 1024 * 1024)),
        name="retention",
    )(log_gamma, proj, proj, proj, cos, sin)


def _ret_out_kernel(o_ref, g_ref, w_ref, res_ref, gain_ref, x_out, h_out):
    acc = None
    for h in range(RET_HEADS):
        cols = slice(h * RET_V_DIM, (h + 1) * RET_V_DIM)
        gate = g_ref[:, cols].astype(F32)
        a = gate * jax.nn.sigmoid(gate) * _rms(o_ref[:, cols].astype(F32), RET_V_DIM)
        d = jnp.dot(a.astype(BF16), w_ref[cols, :], preferred_element_type=F32)
        acc = d if acc is None else acc + d
    _ep_residual(acc, (res_ref, gain_ref), (x_out, h_out))


def _ret_out(o_raw, proj, w_stack, layer, res, gain, *, tm=256):
    m, kdim = o_raw.shape
    n = w_stack.shape[2]
    gate_block = proj.shape[1] // kdim - 1
    blk = (2 * _nbytes((tm, kdim), BF16) + _nbytes((kdim, n), BF16) // 2
           + 2 * _nbytes((tm, n), F32) + _nbytes((tm, n), BF16))
    return pl.pallas_call(
        _ret_out_kernel,
        grid=(m // tm,),
        in_specs=[pl.BlockSpec((tm, kdim), lambda i: (i, 0)),
                  pl.BlockSpec((tm, kdim), lambda i: (i, gate_block)),
                  pl.BlockSpec((None, kdim, n), lambda i: (layer, 0, 0),
                               pipeline_mode=pl.Buffered(1)),
                  pl.BlockSpec((tm, n), lambda i: (i, 0)),
                  pl.BlockSpec((1, n), lambda i: (0, 0))],
        out_specs=[pl.BlockSpec((tm, n), lambda i: (i, 0)), pl.BlockSpec((tm, n), lambda i: (i, 0))],
        out_shape=[jax.ShapeDtypeStruct((m, n), F32), jax.ShapeDtypeStruct((m, n), BF16)],
        compiler_params=pltpu.CompilerParams(
            dimension_semantics=("parallel",),
            vmem_limit_bytes=_vmem_limit(blk, temp_bytes=3 * _nbytes((tm, n), F32))),
        name="ret_out",
    )(o_raw, proj, w_stack, res, gain)


def _flash_kernel(q_ref, k_ref, v_ref, o_ref, m_ref, acc_ref, *, tq, sub, kw, hp, unroll, nq):
    qi = pl.program_id(2)
    nsub = tq // sub
    m_ref[...] = jnp.full_like(m_ref, -jnp.inf)
    acc_ref[...] = jnp.zeros_like(acc_ref)

    def step(hh, si, start, width, diag_offset):
        rows = slice(si * sub, (si + 1) * sub)
        qk_cols = slice(hh * MLA_HEAD_PAD, (hh + 1) * MLA_HEAD_PAD)
        v_cols = slice(hh * MLA_V_PAD, (hh + 1) * MLA_V_PAD)
        q = q_ref[rows, qk_cols]
        k = k_ref[pl.ds(start, width), qk_cols]
        v = v_ref[pl.ds(start, width), v_cols]
        s = lax.dot_general(q, k, (((1,), (1,)), ((), ())), preferred_element_type=F32)
        if diag_offset is not None:
            r = lax.broadcasted_iota(jnp.int32, (sub, width), 0)
            c = lax.broadcasted_iota(jnp.int32, (sub, width), 1)
            s = jnp.where(c <= r + diag_offset, s, NEG_BIG)
        m_prev = m_ref[hh, rows, :]
        m_next = jnp.maximum(m_prev, jnp.max(s, axis=-1, keepdims=True))
        alpha = jnp.exp2(m_prev - m_next)
        p = jnp.exp2(s - jnp.concatenate([m_next] * (width // LANES), axis=1))
        acc_ref[hh, rows, :] = (
            jnp.concatenate([alpha] * (MLA_V_PAD // LANES), axis=1) * acc_ref[hh, rows, :]
            + jnp.dot(p.astype(v.dtype), v, preferred_element_type=F32))
        m_ref[hh, rows, :] = m_next

    def full_chunks(first, count):
        for u in range(count):
            start = pl.multiple_of((first + u) * kw, kw)
            for hh in range(hp):
                for si in range(nsub):
                    step(hh, si, start, kw, None)

    def chunk_group(i, carry):
        full_chunks(i * unroll, unroll)
        return carry

    n_full = qi * (tq // kw)
    lax.fori_loop(0, n_full // unroll, chunk_group, 0)

    def tail(rem):
        full_chunks(n_full - rem, rem)
        tile0 = pl.multiple_of(qi * tq, tq)
        for hh in range(hp):
            for si in range(nsub):
                for kj in range(si + 1):
                    step(hh, si, tile0 + kj * sub, sub, 0 if kj == si else None)
        for hh in range(hp):
            acc = acc_ref[hh]
            o_ref[:, hh * MLA_V_DIM:(hh + 1) * MLA_V_DIM] = (
                acc[:, :MLA_V_DIM] / acc[:, MLA_V_DIM:]).astype(o_ref.dtype)

    for rem in sorted({(i * (tq // kw)) % unroll for i in range(nq)}):
        pl.when(n_full % unroll == rem)(functools.partial(tail, rem))


def _flash(q, k, v, *, batch, seq, heads, tq=2048, sub=512, kw=1024, hp=2, unroll=2):
    m = q.shape[0]
    nq = seq // tq
    blk = (_nbytes((tq, hp * MLA_HEAD_PAD), BF16) + _nbytes((seq, hp * MLA_HEAD_PAD), BF16)
           + _nbytes((seq, hp * MLA_V_PAD), BF16) + _nbytes((tq, hp * MLA_V_DIM), BF16))
    scratch = [pltpu.VMEM((hp, tq, LANES), F32), pltpu.VMEM((hp, tq, MLA_V_PAD), F32)]
    return pl.pallas_call(
        functools.partial(_flash_kernel, tq=tq, sub=sub, kw=kw, hp=hp, unroll=unroll, nq=nq),
        grid=(batch, heads // hp, nq),
        in_specs=[
            pl.BlockSpec((tq, hp * MLA_HEAD_PAD), lambda b, h, i: (b * nq + i, h)),
            pl.BlockSpec((seq, hp * MLA_HEAD_PAD), lambda b, h, i: (b, h)),
            pl.BlockSpec((seq, hp * MLA_V_PAD), lambda b, h, i: (b, h)),
        ],
        out_specs=pl.BlockSpec((tq, hp * MLA_V_DIM), lambda b, h, i: (b * nq + i, h)),
        out_shape=jax.ShapeDtypeStruct((m, heads * MLA_V_DIM), BF16),
        scratch_shapes=scratch,
        compiler_params=pltpu.CompilerParams(
            dimension_semantics=("parallel", "parallel", "arbitrary"),
            vmem_limit_bytes=_vmem_limit(
                blk, scratch_bytes=hp * _nbytes((tq, LANES + MLA_V_PAD), F32),
                temp_bytes=hp * (tq // sub) * _nbytes((sub, kw), F32))),
        name="flash_attention",
    )(q, k, v)


def _rope_tables(positions, dim):
    inv_freq = 1.0 / (ROPE_BASE ** (jnp.arange(0, dim, 2, dtype=F32) / dim))
    ang = positions.astype(F32)[..., None] * inv_freq
    return jnp.cos(ang), jnp.sin(ang)


def kernel(x, positions, norm_mix, norm_mlp, ret_w_in, ret_w_out, kv_norm_in, mla_w_kv_down,
           mla_kv_norm, mla_w_kv_up, mla_k_nope_norm, mla_k_pe_norm, mla_w_dq, mla_q_norm,
           mla_w_uq, mla_q_nope_norm, mla_q_pe_norm, mla_w_o, mlp_w1, mlp_w2):
    batch, seq, d_model = x.shape
    m = batch * seq
    depth = norm_mix.shape[0]
    n_ret = ret_w_in.shape[0]
    mla_heads = mla_w_o.shape[1] // MLA_V_DIM
    row = lambda g: g.reshape(1, -1).astype(F32)

    rc, rs = _rope_tables(positions, RET_QK_DIM)
    ret_cos, ret_sin = rc.reshape(m, -1), rs.reshape(m, -1)
    mc, ms_ = _rope_tables(positions, MLA_ROPE_DIM)
    mc, ms_ = mc.reshape(m, -1), ms_.reshape(m, -1)
    zpad = jnp.zeros((m, LANES - MLA_ROPE_DIM), F32)
    mla_cos = jnp.concatenate([mc, mc, zpad], axis=1)
    mla_sin = jnp.concatenate([-ms_, ms_, zpad], axis=1)
    log_gamma = jnp.log1p(-jnp.exp2(-5.0 - jnp.arange(RET_HEADS, dtype=F32)))
    log_gamma = jnp.broadcast_to(log_gamma[:, None, None], (RET_HEADS, 1, LANES))

    def swap_halves(t):
        half = MLA_ROPE_DIM // 2
        return jnp.concatenate([t[..., half:], t[..., :half]], axis=-1)

    def pe_gain(g):
        g = g.astype(F32)
        return jnp.concatenate([g, swap_halves(g)])[None, :]

    w2_bf, ret_out_bf = mlp_w2.astype(BF16), ret_w_out.astype(BF16)
    w_o_bf, w_dq_bf = mla_w_o.astype(BF16), mla_w_dq.astype(BF16)

    xf = x.reshape(m, d_model)
    h = _rmsnorm(xf, row(norm_mix[0]))
    h_kv = None

    def mlp(xf, h, layer, next_gains):
        u = _mm_wcast(h, mlp_w1, layer, tm=2048, tn=1024, out_dtype=BF16,
                      epilogue=_ep_relu2, name="mlp_up")
        return _mm_residual(u, w2_bf, layer, xf, next_gains, tm=256, name="mlp_down")

    for layer in range(depth):
        if layer < n_ret:
            proj = _mm_wcast(h, ret_w_in, layer, tm=2048, tn=1024, out_dtype=BF16,
                             epilogue=_ep_cast, name="ret_proj")
            r = _retention(proj, ret_cos, ret_sin, log_gamma, batch=batch, seq=seq)
            xf, h = _ret_out(r, proj, ret_out_bf, layer, xf, row(norm_mlp[layer]))
        else:
            j = layer - n_ret
            if j == 0:
                w_down = jnp.concatenate(
                    [mla_w_kv_down, swap_halves(mla_w_kv_down[:, MLA_KV_RANK:])], axis=1).astype(BF16)
                n_down = w_down.shape[1]
                c_kv, k_pe = _matmul(
                    h_kv, w_down, tm=1024, tn=n_down, tk=d_model, epilogue=_ep_kv_down,
                    extra=[row(mla_kv_norm), pe_gain(mla_k_pe_norm), mla_cos, mla_sin],
                    extra_specs=[_gain_spec(MLA_KV_RANK), _gain_spec(LANES),
                                 _row_spec(1024, LANES), _row_spec(1024, LANES)],
                    out_shapes=[jax.ShapeDtypeStruct((m, MLA_KV_RANK), BF16),
                                jax.ShapeDtypeStruct((m, LANES), BF16)],
                    out_specs=[_row_spec(1024, MLA_KV_RANK), _row_spec(1024, LANES)],
                    name="kv_down")
                hpt = 4
                k_all, v_all = _matmul(
                    c_kv, mla_w_kv_up.astype(BF16), tm=1024,
                    tn=hpt * (MLA_NOPE_DIM + MLA_V_DIM), tk=MLA_KV_RANK,
                    epilogue=functools.partial(_ep_kv_up, heads=hpt),
                    extra=[row(mla_k_nope_norm), k_pe],
                    extra_specs=[_gain_spec(MLA_NOPE_DIM), _row_spec(1024, LANES)],
                    out_shapes=[jax.ShapeDtypeStruct((m, mla_heads * MLA_HEAD_PAD), BF16),
                                jax.ShapeDtypeStruct((m, mla_heads * MLA_V_PAD), BF16)],
                    out_specs=[pl.BlockSpec((1024, hpt * MLA_HEAD_PAD), lambda i, j, k: (i, j)),
                               pl.BlockSpec((1024, hpt * MLA_V_PAD), lambda i, j, k: (i, j))],
                    name="kv_up")
            q_rank = mla_w_dq.shape[2]
            c_q = _matmul(h, w_dq_bf, layer=j, tm=1024, tn=q_rank, tk=d_model,
                          epilogue=_ep_norm, extra=[row(mla_q_norm[j])],
                          extra_specs=[_gain_spec(q_rank)],
                          out_shapes=[jax.ShapeDtypeStruct((m, q_rank), BF16)],
                          out_specs=[_row_spec(1024, q_rank)], name="q_down")[0]
            qk_dim = MLA_NOPE_DIM + MLA_ROPE_DIM
            w_uq = mla_w_uq[j].reshape(q_rank, mla_heads, qk_dim)
            w_uq = jnp.concatenate([w_uq, swap_halves(w_uq[..., MLA_NOPE_DIM:])], axis=-1)
            w_uq = w_uq.reshape(q_rank, mla_heads * MLA_HEAD_PAD).astype(BF16)
            q_all = _q_up(c_q, w_uq, row(mla_q_nope_norm[j]), pe_gain(mla_q_pe_norm[j]),
                          mla_cos, mla_sin, scale=qk_dim ** -0.5 * LOG2_E)
            o = _flash(q_all, k_all, v_all, batch=batch, seq=seq, heads=mla_heads)
            xf, h = _mm_residual(o, w_o_bf, j, xf, [row(norm_mlp[layer])], tm=512,
                                 name="attn_out")

        if layer + 1 == depth:
            next_gains = []
        elif layer + 1 == n_ret:
            next_gains = [row(norm_mix[layer + 1]), row(kv_norm_in)]
        else:
            next_gains = [row(norm_mix[layer + 1])]
        outs = mlp(xf, h, layer, next_gains)
        xf = outs[0]
        if layer + 1 == n_ret:
            h, h_kv = outs[1], outs[2]
        elif layer + 1 < depth:
            h = outs[1]

    return xf.reshape(batch, seq, d_model)
```
